```python
import math
import jax, jax.numpy as jnp
from jax import lax
import numpy as np

D_MODEL = 4096
BATCH = 2
SEQ = 4096
DEPTH = 1

MIX_WIDTH = D_MODEL
A_HEADS = 16
NOPE_DIM = 128
ROPE_DIM = 64
V_DIM = 128
A_WIDTH = A_HEADS * V_DIM
Q_LORA = 1024
KV_LORA = 512
ROPE_THETA = 10000.0
Q_BLOCK = 128
G_WIDTH = MIX_WIDTH - A_WIDTH
G_HEADS = 16
G_HEAD_DIM = G_WIDTH // G_HEADS
CHUNK = 128
D_FF = ((8 * D_MODEL // 3 + 255) // 256) * 256
IN_WIDTH = Q_LORA + KV_LORA + ROPE_DIM + 2 * G_WIDTH
EPS = 1e-6

kernel_name = "hybrid_mla_gmlp_sandwich_block"


def rms_norm(x, g):
    xf = x.astype(jnp.float32)
    y = xf * lax.rsqrt(jnp.mean(xf * xf, axis=-1, keepdims=True) + EPS)
    return (y * g.astype(jnp.float32)).astype(x.dtype)


def layer_norm(x, g, b):
    xf = x.astype(jnp.float32)
    mu = jnp.mean(xf, axis=-1, keepdims=True)
    xc = xf - mu
    y = xc * lax.rsqrt(jnp.mean(xc * xc, axis=-1, keepdims=True) + EPS)
    return (y * g.astype(jnp.float32) + b.astype(jnp.float32)).astype(x.dtype)


def rope_tables(positions, dtype):
    inv_freq = 1.0 / (ROPE_THETA ** (jnp.arange(0, ROPE_DIM, 2, dtype=jnp.float32) / ROPE_DIM))
    ang = positions.astype(jnp.float32)[..., None] * inv_freq
    return jnp.cos(ang).astype(dtype), jnp.sin(ang).astype(dtype)


def apply_rope(t, cos, sin):
    half = t.shape[-1] // 2
    t1, t2 = t[..., :half], t[..., half:]
    return jnp.concatenate([t1 * cos - t2 * sin, t2 * cos + t1 * sin], axis=-1)


def mla_attention(q_nope, q_rope, k_nope, k_rope, v):
    B, S, H, _ = q_nope.shape
    nb = S // Q_BLOCK
    scale = 1.0 / math.sqrt(NOPE_DIM + ROPE_DIM)
    qn = q_nope.reshape(B, nb, Q_BLOCK, H, NOPE_DIM).transpose(1, 0, 2, 3, 4)
    qr = q_rope.reshape(B, nb, Q_BLOCK, H, ROPE_DIM).transpose(1, 0, 2, 3, 4)

    def block(args):
        qn_b, qr_b = args
        s = (jnp.einsum('bqhd,bkhd->bhqk', qn_b, k_nope, preferred_element_type=jnp.float32)
             + jnp.einsum('bqhr,bkr->bhqk', qr_b, k_rope, preferred_element_type=jnp.float32))
        p = jax.nn.softmax(s * scale, axis=-1).astype(v.dtype)
        return jnp.einsum('bhqk,bkhd->bqhd', p, v)

    out = lax.map(block, (qn, qr))
    return out.transpose(1, 0, 2, 3, 4).reshape(B, S, H * V_DIM)


def spatial_gating(u, v, v_ln_g, v_ln_b, w_s, b_s):
    B, S, _ = v.shape
    v = layer_norm(v, v_ln_g, v_ln_b)
    vc = v.reshape(B, S // CHUNK, CHUNK, G_HEADS, G_HEAD_DIM)
    s = jnp.einsum('gpq,bcqgd->bcpgd', w_s, vc) + jnp.transpose(b_s)[None, None, :, :, None]
    return u * s.reshape(B, S, G_WIDTH)


def setup_inputs(seed: int = 0) -> dict:
    key = jax.random.key(seed)
    ks = jax.random.split(key, 24)
    f32 = jnp.float32
    L = DEPTH

    def nrm(k, shape, fan_in):
        return jax.random.normal(k, shape, f32) * (fan_in ** -0.5)

    def gain(k, n):
        return 1.0 + 0.02 * jax.random.normal(k, (L, n), f32)

    x = jax.random.normal(ks[0], (BATCH, SEQ, D_MODEL), f32)
    offs = jax.random.randint(ks[1], (BATCH, 1), 0, SEQ, dtype=jnp.int32)
    positions = jnp.arange(SEQ, dtype=jnp.int32)[None, :] + offs
    return {
        "x": x,
        "positions": positions,
        "pre_mix_norm": gain(ks[2], D_MODEL),
        "w_in": nrm(ks[3], (L, D_MODEL, IN_WIDTH), D_MODEL),
        "q_norm": gain(ks[4], Q_LORA),
        "kv_norm": gain(ks[5], KV_LORA),
        "w_uq": nrm(ks[6], (L, Q_LORA, A_HEADS * (NOPE_DIM + ROPE_DIM)), Q_LORA),
        "w_ukv": nrm(ks[7], (L, KV_LORA, A_HEADS * (NOPE_DIM + V_DIM)), KV_LORA),
        "v_ln_gain": gain(ks[8], G_WIDTH),
        "v_ln_bias": 0.02 * jax.random.normal(ks[9], (L, G_WIDTH), f32),
        "w_spatial": nrm(ks[10], (L, G_HEADS, CHUNK, CHUNK), CHUNK),
        "b_spatial": 1.0 + 0.1 * jax.random.normal(ks[11], (L, G_HEADS, CHUNK), f32),
        "attn_out_norm": gain(ks[12], A_WIDTH),
        "gmlp_out_norm": gain(ks[13], G_WIDTH),
        "w_out": nrm(ks[14], (L, MIX_WIDTH, D_MODEL), MIX_WIDTH),
        "post_mix_norm": gain(ks[15], D_MODEL),
        "pre_ffn_norm": gain(ks[16], D_MODEL),
        "w_gate": nrm(ks[17], (L, D_MODEL, D_FF), D_MODEL),
        "w_up": nrm(ks[18], (L, D_MODEL, D_FF), D_MODEL),
        "w_down": nrm(ks[19], (L, D_FF, D_MODEL), D_FF),
        "post_ffn_norm": gain(ks[20], D_MODEL),
    }


def reference(x, positions, pre_mix_norm, w_in, q_norm, kv_norm, w_uq, w_ukv,
              v_ln_gain, v_ln_bias, w_spatial, b_spatial, attn_out_norm, gmlp_out_norm,
              w_out, post_mix_norm, pre_ffn_norm, w_gate, w_up, w_down, post_ffn_norm):
    B, S, _ = x.shape
    cos, sin = rope_tables(positions, x.dtype)
    splits = np.cumsum([Q_LORA, KV_LORA, ROPE_DIM, G_WIDTH]).tolist()

    for l in range(DEPTH):
        xn = rms_norm(x, pre_mix_norm[l])
        proj = jnp.einsum('bsd,de->bse', xn, w_in[l])
        q_c, kv_c, k_rope, u, v = jnp.split(proj, splits, axis=-1)

        q = jnp.einsum('bsr,re->bse', rms_norm(q_c, q_norm[l]), w_uq[l])
        q = q.reshape(B, S, A_HEADS, NOPE_DIM + ROPE_DIM)
        q_nope = q[..., :NOPE_DIM]
        q_rope = apply_rope(q[..., NOPE_DIM:], cos[:, :, None, :], sin[:, :, None, :])
        kv = jnp.einsum('bsr,re->bse', rms_norm(kv_c, kv_norm[l]), w_ukv[l])
        kv = kv.reshape(B, S, A_HEADS, NOPE_DIM + V_DIM)
        k_nope, v_a = kv[..., :NOPE_DIM], kv[..., NOPE_DIM:]
        k_rope = apply_rope(k_rope, cos, sin)
        a_out = mla_attention(q_nope, q_rope, k_nope, k_rope, v_a)

        u = jax.nn.gelu(u)
        v = jax.nn.gelu(v)
        g_out = spatial_gating(u, v, v_ln_gain[l], v_ln_bias[l], w_spatial[l], b_spatial[l])

        mixed = jnp.concatenate([rms_norm(a_out, attn_out_norm[l]),
                                 rms_norm(g_out, gmlp_out_norm[l])], axis=-1)
        mix_out = jnp.einsum('bse,ed->bsd', mixed, w_out[l])
        x = x + rms_norm(mix_out, post_mix_norm[l])

        hn = rms_norm(x, pre_ffn_norm[l])
        gate = jnp.einsum('bsd,df->bsf', hn, w_gate[l])
        up = jnp.einsum('bsd,df->bsf', hn, w_up[l])
        ffn = jnp.einsum('bsf,fd->bsd', jax.nn.silu(gate) * up, w_down[l])
        x = x + rms_norm(ffn, post_ffn_norm[l])

    return x
```

```python
import functools
import math

import jax
import jax.numpy as jnp
from jax import lax
from jax.experimental import pallas as pl
from jax.experimental.pallas import tpu as pltpu

F32 = jnp.float32
BF16 = jnp.bfloat16

D_MODEL = 4096
A_HEADS = 16
NOPE_DIM = 128
ROPE_DIM = 64
HALF_ROPE = ROPE_DIM // 2
V_DIM = 128
A_WIDTH = A_HEADS * V_DIM
Q_LORA = 1024
KV_LORA = 512
ROPE_THETA = 10000.0
G_WIDTH = D_MODEL - A_WIDTH
G_HEADS = 16
G_HEAD_DIM = G_WIDTH // G_HEADS
CHUNK = 128
EPS = 1e-6

LANES = 128
QK_PAD = 2 * LANES
MIB = 1024 * 1024

Q_PRESCALE = (1.0 / math.sqrt(NOPE_DIM + ROPE_DIM)) * math.log2(math.e)


def _params(semantics, vmem_mib):
    return pltpu.CompilerParams(dimension_semantics=semantics,
                                vmem_limit_bytes=int(vmem_mib * MIB))


def _rms(x, g):
    return x * lax.rsqrt(jnp.mean(x * x, axis=-1, keepdims=True) + EPS) * g


def _rmsnorm_kernel(x_ref, g_ref, o_ref):
    o_ref[...] = _rms(x_ref[...].astype(F32), g_ref[...]).astype(o_ref.dtype)


def _rmsnorm(x, g, out_dtype, tm=256):
    m, d = x.shape
    return pl.pallas_call(
        _rmsnorm_kernel,
        grid=(m // tm,),
        in_specs=[pl.BlockSpec((tm, d), lambda i: (i, 0)),
                  pl.BlockSpec((1, d), lambda i: (0, 0))],
        out_specs=pl.BlockSpec((tm, d), lambda i: (i, 0)),
        out_shape=jax.ShapeDtypeStruct((m, d), out_dtype),
        compiler_params=_params(("parallel",), 40),
        name="rmsnorm",
    )(x, g)


def _post_mix_kernel(x_ref, mix_ref, g1_ref, g2_ref, h_ref, hn_ref):
    h = x_ref[...] + _rms(mix_ref[...].astype(F32), g1_ref[...])
    h_ref[...] = h
    hn_ref[...] = _rms(h, g2_ref[...]).astype(hn_ref.dtype)


def _post_mix(x, mix, g1, g2, tm=256):
    m, d = x.shape
    row = pl.BlockSpec((tm, d), lambda i: (i, 0))
    gain = pl.BlockSpec((1, d), lambda i: (0, 0))
    return pl.pallas_call(
        _post_mix_kernel,
        grid=(m // tm,),
        in_specs=[row, row, gain, gain],
        out_specs=[row, row],
        out_shape=[jax.ShapeDtypeStruct((m, d), F32),
                   jax.ShapeDtypeStruct((m, d), BF16)],
        compiler_params=_params(("parallel",), 48),
        name="post_mix",
    )(x, mix, g1, g2)


def _post_ffn_kernel(h_ref, f_ref, g_ref, o_ref):
    o_ref[...] = h_ref[...] + _rms(f_ref[...].astype(F32), g_ref[...])


def _post_ffn(h, ffn, g, tm=256):
    m, d = h.shape
    row = pl.BlockSpec((tm, d), lambda i: (i, 0))
    return pl.pallas_call(
        _post_ffn_kernel,
        grid=(m // tm,),
        in_specs=[row, row, pl.BlockSpec((1, d), lambda i: (0, 0))],
        out_specs=row,
        out_shape=jax.ShapeDtypeStruct((m, d), F32),
        compiler_params=_params(("parallel",), 48),
        name="post_ffn",
    )(h, ffn, g)


def _mm_kernel(a_ref, w_ref, o_ref, *, epilogue):
    acc = jnp.dot(a_ref[...], w_ref[...], preferred_element_type=F32)
    if epilogue is not None:
        acc = epilogue(acc)
    o_ref[...] = acc.astype(o_ref.dtype)


def _matmul(a, w, *, tm, tn, out_dtype, epilogue=None, vmem_mib=48, name="matmul"):
    m, k = a.shape
    _, n = w.shape
    return pl.pallas_call(
        functools.partial(_mm_kernel, epilogue=epilogue),
        grid=(m // tm, n // tn),
        in_specs=[pl.BlockSpec((tm, k), lambda i, j: (i, 0)),
                  pl.BlockSpec((k, tn), lambda i, j: (0, j))],
        out_specs=pl.BlockSpec((tm, tn), lambda i, j: (i, j)),
        out_shape=jax.ShapeDtypeStruct((m, n), out_dtype),
        compiler_params=_params(("parallel", "parallel"), vmem_mib),
        name=name,
    )(a, w)


def _mm2_kernel(a1_ref, a2_ref, w1_ref, w2_ref, o_ref):
    acc = jnp.dot(a1_ref[...], w1_ref[...], preferred_element_type=F32)
    acc = acc + jnp.dot(a2_ref[...], w2_ref[...], preferred_element_type=F32)
    o_ref[...] = acc.astype(o_ref.dtype)


def _matmul2(a1, a2, w1, w2, *, tm, tn, out_dtype, vmem_mib=48, name="matmul2"):
    m, k1 = a1.shape
    _, k2 = a2.shape
    n = w1.shape[1]
    return pl.pallas_call(
        _mm2_kernel,
        grid=(m // tm, n // tn),
        in_specs=[pl.BlockSpec((tm, k1), lambda i, j: (i, 0)),
                  pl.BlockSpec((tm, k2), lambda i, j: (i, 0)),
                  pl.BlockSpec((k1, tn), lambda i, j: (0, j)),
                  pl.BlockSpec((k2, tn), lambda i, j: (0, j))],
        out_specs=pl.BlockSpec((tm, tn), lambda i, j: (i, j)),
        out_shape=jax.ShapeDtypeStruct((m, n), out_dtype),
        compiler_params=_params(("parallel", "parallel"), vmem_mib),
        name=name,
    )(a1, a2, w1, w2)


def _gate_up_kernel(a_ref, wg_ref, wu_ref, o_ref):
    a = a_ref[...]
    g = jnp.dot(a, wg_ref[...], preferred_element_type=F32)
    u = jnp.dot(a, wu_ref[...], preferred_element_type=F32)
    o_ref[...] = (jax.nn.silu(g) * u).astype(o_ref.dtype)


def _gate_up(a, wg, wu, *, tm, tn, vmem_mib=48):
    m, k = a.shape
    n = wg.shape[1]
    wspec = pl.BlockSpec((k, tn), lambda i, j: (0, j))
    return pl.pallas_call(
        _gate_up_kernel,
        grid=(m // tm, n // tn),
        in_specs=[pl.BlockSpec((tm, k), lambda i, j: (i, 0)), wspec, wspec],
        out_specs=pl.BlockSpec((tm, tn), lambda i, j: (i, j)),
        out_shape=jax.ShapeDtypeStruct((m, n), BF16),
        compiler_params=_params(("parallel", "parallel"), vmem_mib),
        name="gate_up",
    )(a, wg, wu)


def _q_down_kernel(xn_ref, w_ref, g_ref, o_ref):
    acc = jnp.dot(xn_ref[...], w_ref[...], preferred_element_type=F32)
    o_ref[...] = _rms(acc, g_ref[...]).astype(o_ref.dtype)


def _q_down(xn, w_q, g, tm=512):
    m, k = xn.shape
    n = w_q.shape[1]
    return pl.pallas_call(
        _q_down_kernel,
        grid=(m // tm,),
        in_specs=[pl.BlockSpec((tm, k), lambda i: (i, 0)),
                  pl.BlockSpec((k, n), lambda i: (0, 0)),
                  pl.BlockSpec((1, n), lambda i: (0, 0))],
        out_specs=pl.BlockSpec((tm, n), lambda i: (i, 0)),
        out_shape=jax.ShapeDtypeStruct((m, n), BF16),
        compiler_params=_params(("parallel",), 40),
        name="q_down",
    )(xn, w_q, g)


def _q_up_kernel(qn_ref, pos_ref, wuq_ref, wsw_ref, invf_ref, sgn_ref, o_ref):
    qn = qn_ref[...]
    qm = jnp.dot(qn, wuq_ref[...], preferred_element_type=F32)
    qs = jnp.dot(qn, wsw_ref[...], preferred_element_type=F32)
    ang = pos_ref[...].astype(F32) * invf_ref[...]
    cos_b = jnp.cos(ang)
    sin_b = jnp.sin(ang) * sgn_ref[...]
    for h in range(A_HEADS):
        lo = h * QK_PAD
        o_ref[:, lo:lo + LANES] = (qm[:, lo:lo + LANES] * Q_PRESCALE).astype(o_ref.dtype)
        rot = qm[:, lo + LANES:lo + QK_PAD] * cos_b + qs[:, h * LANES:(h + 1) * LANES] * sin_b
        o_ref[:, lo + LANES:lo + QK_PAD] = (rot * Q_PRESCALE).astype(o_ref.dtype)


def _q_up(qn, pos_col, wuq, wsw, invf_row, sgn_row, tm=512):
    m, r = qn.shape
    n = A_HEADS * QK_PAD
    const = lambda i: (0, 0)
    return pl.pallas_call(
        _q_up_kernel,
        grid=(m // tm,),
        in_specs=[pl.BlockSpec((tm, r), lambda i: (i, 0)),
                  pl.BlockSpec((tm, 1), lambda i: (i, 0)),
                  pl.BlockSpec(wuq.shape, const),
                  pl.BlockSpec(wsw.shape, const),
                  pl.BlockSpec((1, LANES), const),
                  pl.BlockSpec((1, LANES), const)],
        out_specs=pl.BlockSpec((tm, n), lambda i: (i, 0)),
        out_shape=jax.ShapeDtypeStruct((m, n), BF16),
        compiler_params=_params(("parallel",), 56),
        name="q_up",
    )(qn, pos_col, wuq, wsw, invf_row, sgn_row)


def _kv_down_kernel(xn_ref, w_ref, g_ref, kvn_ref, kr_ref):
    acc = jnp.dot(xn_ref[...], w_ref[...], preferred_element_type=F32)
    kvn_ref[...] = _rms(acc[:, :KV_LORA], g_ref[...]).astype(kvn_ref.dtype)
    kr_ref[...] = acc[:, KV_LORA:]


def _kv_down(xn, w_kvr, g, tm=512):
    m, k = xn.shape
    n = w_kvr.shape[1]
    return pl.pallas_call(
        _kv_down_kernel,
        grid=(m // tm,),
        in_specs=[pl.BlockSpec((tm, k), lambda i: (i, 0)),
                  pl.BlockSpec((k, n), lambda i: (0, 0)),
                  pl.BlockSpec((1, KV_LORA), lambda i: (0, 0))],
        out_specs=[pl.BlockSpec((tm, KV_LORA), lambda i: (i, 0)),
                   pl.BlockSpec((tm, n - KV_LORA), lambda i: (i, 0))],
        out_shape=[jax.ShapeDtypeStruct((m, KV_LORA), BF16),
                   jax.ShapeDtypeStruct((m, n - KV_LORA), F32)],
        compiler_params=_params(("parallel",), 40),
        name="kv_down",
    )(xn, w_kvr, g)


def _kv_up_kernel(kvn_ref, kr_ref, pos_ref, wkt_ref, wv_ref, invf_ref, kt_ref, v_ref):
    kvn = kvn_ref[...]
    knt = lax.dot_general(wkt_ref[...], kvn, (((1,), (1,)), ((), ())),
                          preferred_element_type=F32)
    v_ref[...] = jnp.dot(kvn, wv_ref[...], preferred_element_type=F32).astype(v_ref.dtype)
    krt = kr_ref[...].T
    ang = invf_ref[...] * pos_ref[...].astype(F32)
    cos_t = jnp.cos(ang)
    sin_t = jnp.sin(ang)
    t1 = krt[:HALF_ROPE]
    t2 = krt[HALF_ROPE:ROPE_DIM]
    rope_rows = jnp.concatenate(
        [t1 * cos_t - t2 * sin_t, t2 * cos_t + t1 * sin_t,
         jnp.zeros((QK_PAD - NOPE_DIM - ROPE_DIM, krt.shape[1]), F32)], axis=0
    ).astype(kt_ref.dtype)
    for h in range(A_HEADS):
        lo = h * QK_PAD
        kt_ref[lo:lo + NOPE_DIM, :] = knt[h * NOPE_DIM:(h + 1) * NOPE_DIM].astype(kt_ref.dtype)
        kt_ref[lo + NOPE_DIM:lo + QK_PAD, :] = rope_rows


def _kv_up(kvn, kr, pos_row, wkt, wv, invf_col, batch, seq, tm=512):
    m = kvn.shape[0]
    spb = seq // tm
    const = lambda i: (0, 0)
    return pl.pallas_call(
        _kv_up_kernel,
        grid=(m // tm,),
        in_specs=[pl.BlockSpec((tm, KV_LORA), lambda i: (i, 0)),
                  pl.BlockSpec((tm, LANES), lambda i: (i, 0)),
                  pl.BlockSpec((1, tm), lambda i: (0, i)),
                  pl.BlockSpec(wkt.shape, const),
                  pl.BlockSpec(wv.shape, const),
                  pl.BlockSpec((HALF_ROPE, 1), const)],
        out_specs=[pl.BlockSpec((None, A_HEADS * QK_PAD, tm), lambda i: (i // spb, 0, i % spb)),
                   pl.BlockSpec((tm, A_WIDTH), lambda i: (i, 0))],
        out_shape=[jax.ShapeDtypeStruct((batch, A_HEADS * QK_PAD, seq), BF16),
                   jax.ShapeDtypeStruct((m, A_WIDTH), BF16)],
        compiler_params=_params(("parallel",), 48),
        name="kv_up",
    )(kvn, kr, pos_row, wkt, wv, invf_col)


def _attn_kernel(q_ref, kt_ref, v_ref, o_ref, *, tk):
    q = q_ref[...]
    tq = q.shape[0]
    seq = kt_ref.shape[-1]
    m = jnp.full((tq, 1), -jnp.inf, F32)
    l = jnp.zeros((tq, 1), F32)
    acc = jnp.zeros((tq, V_DIM), F32)
    for c in range(seq // tk):
        s = jnp.dot(q, kt_ref[:, c * tk:(c + 1) * tk], preferred_element_type=F32)
        m_new = jnp.maximum(m, jnp.max(s, axis=-1, keepdims=True))
        alpha = jnp.exp2(m - m_new)
        p = jnp.exp2(s - m_new)
        l = alpha * l + jnp.sum(p, axis=-1, keepdims=True)
        acc = alpha * acc + jnp.dot(p.astype(BF16), v_ref[c * tk:(c + 1) * tk, :],
                                    preferred_element_type=F32)
        m = m_new
    o_ref[...] = (acc / l).astype(o_ref.dtype)


def _attention(q, kt, v, batch, seq, tq=512, tk=512):
    nq = seq // tq
    return pl.pallas_call(
        functools.partial(_attn_kernel, tk=tk),
        grid=(batch, A_HEADS, nq),
        in_specs=[pl.BlockSpec((tq, QK_PAD), lambda b, h, i: (b * nq + i, h)),
                  pl.BlockSpec((None, QK_PAD, seq), lambda b, h, i: (b, h, 0)),
                  pl.BlockSpec((seq, V_DIM), lambda b, h, i: (b, h))],
        out_specs=pl.BlockSpec((tq, V_DIM), lambda b, h, i: (b * nq + i, h)),
        out_shape=jax.ShapeDtypeStruct((batch * seq, A_WIDTH), BF16),
        compiler_params=_params(("parallel", "parallel", "parallel"), 40),
        name="attention",
    )(q, kt, v)


def _gating_kernel(uv_ref, lng_ref, lnb_ref, ws_ref, bs_ref, og_ref, o_ref, go_ref):
    v = uv_ref[:, G_WIDTH:].astype(F32)
    mu = jnp.mean(v, axis=-1, keepdims=True)
    vc = v - mu
    vln = (vc * lax.rsqrt(jnp.mean(vc * vc, axis=-1, keepdims=True) + EPS) * lng_ref[...]
           + lnb_ref[...]).astype(BF16)
    rows = uv_ref.shape[0]
    for g in range(G_HEADS):
        cols = slice(g * G_HEAD_DIM, (g + 1) * G_HEAD_DIM)
        w = ws_ref[g]
        b = bs_ref[g]
        for c in range(rows // CHUNK):
            rs = slice(c * CHUNK, (c + 1) * CHUNK)
            s = jnp.dot(w, vln[rs, cols], preferred_element_type=F32) + b
            go_ref[rs, cols] = uv_ref[rs, cols].astype(F32) * s
    o_ref[...] = _rms(go_ref[...], og_ref[...]).astype(o_ref.dtype)


def _gating(uv, ln_g, ln_b, ws, bs, out_g, tm=512):
    m = uv.shape[0]
    const2 = lambda i: (0, 0)
    const3 = lambda i: (0, 0, 0)
    return pl.pallas_call(
        _gating_kernel,
        grid=(m // tm,),
        in_specs=[pl.BlockSpec((tm, 2 * G_WIDTH), lambda i: (i, 0)),
                  pl.BlockSpec((1, G_WIDTH), const2),
                  pl.BlockSpec((1, G_WIDTH), const2),
                  pl.BlockSpec(ws.shape, const3),
                  pl.BlockSpec(bs.shape, const3),
                  pl.BlockSpec((1, G_WIDTH), const2)],
        out_specs=pl.BlockSpec((tm, G_WIDTH), lambda i: (i, 0)),
        out_shape=jax.ShapeDtypeStruct((m, G_WIDTH), BF16),
        scratch_shapes=[pltpu.VMEM((tm, G_WIDTH), F32)],
        compiler_params=_params(("parallel",), 48),
        name="gating",
    )(uv, ln_g, ln_b, ws, bs, out_g)


def kernel(x, positions, pre_mix_norm, w_in, q_norm, kv_norm, w_uq, w_ukv, v_ln_gain, v_ln_bias,
           w_spatial, b_spatial, attn_out_norm, gmlp_out_norm, w_out, post_mix_norm, pre_ffn_norm,
           w_gate, w_up, w_down, post_ffn_norm):
    batch, seq, d = x.shape
    tokens = batch * seq
    depth = w_in.shape[0]

    inv_freq = 1.0 / (ROPE_THETA ** (jnp.arange(0, ROPE_DIM, 2, dtype=F32) / ROPE_DIM))
    zeros_half = jnp.zeros((LANES - ROPE_DIM,), F32)
    invf_row = jnp.concatenate([inv_freq, inv_freq, zeros_half])[None, :]
    sgn_row = jnp.concatenate([-jnp.ones((HALF_ROPE,), F32), jnp.ones((HALF_ROPE,), F32),
                               zeros_half])[None, :]
    invf_col = inv_freq[:, None]
    pos_col = positions.reshape(tokens, 1)
    pos_row = positions.reshape(1, tokens)

    xf = x.reshape(tokens, d)
    for l in range(depth):
        wi = w_in[l]
        o_kv, o_kr, o_u = Q_LORA, Q_LORA + KV_LORA, Q_LORA + KV_LORA + ROPE_DIM
        w_q = wi[:, :o_kv].astype(BF16)
        w_kvr = jnp.concatenate(
            [wi[:, o_kv:o_u], jnp.zeros((d, LANES - ROPE_DIM), F32)], axis=1).astype(BF16)
        w_uv = wi[:, o_u:].astype(BF16)
        wq3 = w_uq[l].reshape(Q_LORA, A_HEADS, NOPE_DIM + ROPE_DIM)
        rope3 = wq3[:, :, NOPE_DIM:]
        pad3 = jnp.zeros((Q_LORA, A_HEADS, LANES - ROPE_DIM), F32)
        wuq_pad = jnp.concatenate([wq3[:, :, :NOPE_DIM], rope3, pad3], axis=2
                                  ).reshape(Q_LORA, A_HEADS * QK_PAD).astype(BF16)
        wsw_pad = jnp.concatenate([rope3[:, :, HALF_ROPE:], rope3[:, :, :HALF_ROPE], pad3], axis=2
                                  ).reshape(Q_LORA, A_HEADS * LANES).astype(BF16)
        wkv3 = w_ukv[l].reshape(KV_LORA, A_HEADS, NOPE_DIM + V_DIM)
        wk_t = wkv3[:, :, :NOPE_DIM].reshape(KV_LORA, A_HEADS * NOPE_DIM).T.astype(BF16)
        wv = wkv3[:, :, NOPE_DIM:].reshape(KV_LORA, A_WIDTH).astype(BF16)
        ws = w_spatial[l].astype(BF16)
        bs = b_spatial[l][:, :, None]
        w_o = w_out[l].astype(BF16)
        w_g = w_gate[l].astype(BF16)
        w_u = w_up[l].astype(BF16)
        w_d = w_down[l].astype(BF16)

        xn = _rmsnorm(xf, pre_mix_norm[l][None, :], BF16)
        qn = _q_down(xn, w_q, q_norm[l][None, :])
        q = _q_up(qn, pos_col, wuq_pad, wsw_pad, invf_row, sgn_row)
        kvn, kr = _kv_down(xn, w_kvr, kv_norm[l][None, :])
        kt, v_a = _kv_up(kvn, kr, pos_row, wk_t, wv, invf_col, batch, seq)
        a_out = _attention(q, kt, v_a, batch, seq)
        a_n = _rmsnorm(a_out, attn_out_norm[l][None, :], BF16)

        uv = _matmul(xn, w_uv, tm=1024, tn=512, out_dtype=BF16, epilogue=jax.nn.gelu, name="uv_proj")
        g_n = _gating(uv, v_ln_gain[l][None, :], v_ln_bias[l][None, :], ws, bs,
                      gmlp_out_norm[l][None, :])

        mix = _matmul2(a_n, g_n, w_o[:A_WIDTH], w_o[A_WIDTH:], tm=1024, tn=512, out_dtype=BF16,
                       name="out_proj")
        h, hn = _post_mix(xf, mix, post_mix_norm[l][None, :], pre_ffn_norm[l][None, :])

        act = _gate_up(hn, w_g, w_u, tm=1024, tn=256)
        ffn = _matmul(act, w_d, tm=512, tn=512, out_dtype=BF16, vmem_mib=56, name="down_proj")
        xf = _post_ffn(h, ffn, post_ffn_norm[l][None, :])

    return xf.reshape(batch, seq, d)
```

```python
import functools
import math

import jax
import jax.numpy as jnp
from jax import lax
from jax.experimental import pallas as pl
from jax.experimental.pallas import tpu as pltpu

F32 = jnp.float32
BF16 = jnp.bfloat16

D_MODEL = 4096
A_HEADS = 16
NOPE_DIM = 128
ROPE_DIM = 64
HALF_ROPE = ROPE_DIM // 2
QK_DIM = NOPE_DIM + ROPE_DIM
V_DIM = 128
A_WIDTH = A_HEADS * V_DIM
Q_LORA = 1024
KV_LORA = 512
ROPE_THETA = 10000.0
G_WIDTH = D_MODEL - A_WIDTH
G_HEADS = 16
G_HEAD_DIM = G_WIDTH // G_HEADS
CHUNK = 128
EPS = 1e-6

LANES = 128
SUBLANES = 8
QK_PAD = 2 * LANES
MIB = 1024 * 1024

Q_PRESCALE = (1.0 / math.sqrt(QK_DIM)) * math.log2(math.e)

_NT = (((1,), (1,)), ((), ()))


def _params(semantics, vmem_mib):
    return pltpu.CompilerParams(dimension_semantics=semantics,
                                vmem_limit_bytes=int(vmem_mib * MIB))


def _rms(x, g):
    return x * lax.rsqrt(jnp.mean(x * x, axis=-1, keepdims=True) + EPS) * g


def _post_mix_kernel(x_ref, mix_ref, g1_ref, g2_ref, h_ref, hn_ref):
    h = x_ref[...] + _rms(mix_ref[...].astype(F32), g1_ref[...])
    h_ref[...] = h
    hn_ref[...] = _rms(h, g2_ref[...]).astype(hn_ref.dtype)


def _post_mix(x, mix, g1, g2, tm=256):
    m, d = x.shape
    row = pl.BlockSpec((tm, d), lambda i: (i, 0))
    gain = pl.BlockSpec((1, d), lambda i: (0, 0))
    return pl.pallas_call(
        _post_mix_kernel,
        grid=(m // tm,),
        in_specs=[row, row, gain, gain],
        out_specs=[row, row],
        out_shape=[jax.ShapeDtypeStruct((m, d), F32),
                   jax.ShapeDtypeStruct((m, d), BF16)],
        compiler_params=_params(("parallel",), 48),
        name="post_mix",
    )(x, mix, g1, g2)


def _post_ffn_kernel(h_ref, f_ref, g_ref, o_ref):
    o_ref[...] = h_ref[...] + _rms(f_ref[...].astype(F32), g_ref[...])


def _post_ffn(h, ffn, g, tm=256):
    m, d = h.shape
    row = pl.BlockSpec((tm, d), lambda i: (i, 0))
    return pl.pallas_call(
        _post_ffn_kernel,
        grid=(m // tm,),
        in_specs=[row, row, pl.BlockSpec((1, d), lambda i: (0, 0))],
        out_specs=row,
        out_shape=jax.ShapeDtypeStruct((m, d), F32),
        compiler_params=_params(("parallel",), 48),
        name="post_ffn",
    )(h, ffn, g)


def _qkv_down_kernel(x_ref, pg_ref, w_ref, qg_ref, kg_ref, qn_ref, kvn_ref, kr_ref):
    xn = _rms(x_ref[...], pg_ref[...]).astype(BF16)
    acc = jnp.dot(xn, w_ref[...], preferred_element_type=F32)
    qn_ref[...] = _rms(acc[:, :Q_LORA], qg_ref[...]).astype(qn_ref.dtype)
    kvn_ref[...] = _rms(acc[:, Q_LORA:Q_LORA + KV_LORA], kg_ref[...]).astype(kvn_ref.dtype)
    kr_ref[...] = acc[:, Q_LORA + KV_LORA:]


def _qkv_down(x, pre_g, w_qkv, q_g, kv_g, tm=256):
    m, d = x.shape
    n = w_qkv.shape[1]
    n_kr = n - Q_LORA - KV_LORA
    const = lambda i: (0, 0)
    return pl.pallas_call(
        _qkv_down_kernel,
        grid=(m // tm,),
        in_specs=[pl.BlockSpec((tm, d), lambda i: (i, 0)),
                  pl.BlockSpec((1, d), const),
                  pl.BlockSpec((d, n), const),
                  pl.BlockSpec((1, Q_LORA), const),
                  pl.BlockSpec((1, KV_LORA), const)],
        out_specs=[pl.BlockSpec((tm, Q_LORA), lambda i: (i, 0)),
                   pl.BlockSpec((tm, KV_LORA), lambda i: (i, 0)),
                   pl.BlockSpec((tm, n_kr), lambda i: (i, 0))],
        out_shape=[jax.ShapeDtypeStruct((m, Q_LORA), BF16),
                   jax.ShapeDtypeStruct((m, KV_LORA), BF16),
                   jax.ShapeDtypeStruct((m, n_kr), F32)],
        compiler_params=_params(("parallel",), 52),
        name="qkv_down",
    )(x, pre_g, w_qkv, q_g, kv_g)


def _uv_proj_kernel(x_ref, pg_ref, w_ref, o_ref, xn_ref):
    @pl.when(pl.program_id(1) == 0)
    def _():
        xn_ref[...] = _rms(x_ref[...], pg_ref[...]).astype(xn_ref.dtype)

    acc = jnp.dot(xn_ref[...], w_ref[...], preferred_element_type=F32)
    o_ref[...] = jax.nn.gelu(acc).astype(o_ref.dtype)


def _uv_proj(x, pre_g, w_uv, tm=512, tn=1024):
    m, d = x.shape
    n = w_uv.shape[1]
    return pl.pallas_call(
        _uv_proj_kernel,
        grid=(m // tm, n // tn),
        in_specs=[pl.BlockSpec((tm, d), lambda i, j: (i, 0)),
                  pl.BlockSpec((1, d), lambda i, j: (0, 0)),
                  pl.BlockSpec((d, tn), lambda i, j: (0, j))],
        out_specs=pl.BlockSpec((tm, tn), lambda i, j: (i, j)),
        out_shape=jax.ShapeDtypeStruct((m, n), BF16),
        scratch_shapes=[pltpu.VMEM((tm, d), BF16)],
        compiler_params=_params(("parallel", "arbitrary"), 52),
        name="uv_proj",
    )(x, pre_g, w_uv)


def _q_up_kernel(qn_ref, pos_ref, wt_ref, invf_ref, o_ref):
    qt = lax.dot_general(wt_ref[...], qn_ref[...], _NT, preferred_element_type=F32)
    ang = invf_ref[...] * pos_ref[...].astype(F32)
    cos_t = jnp.cos(ang)
    sin_t = jnp.sin(ang)
    zeros = jnp.zeros((QK_PAD - QK_DIM, qt.shape[1]), o_ref.dtype)
    for h in range(A_HEADS):
        src = h * QK_DIM
        dst = h * QK_PAD
        t1 = qt[src + NOPE_DIM:src + NOPE_DIM + HALF_ROPE]
        t2 = qt[src + NOPE_DIM + HALF_ROPE:src + QK_DIM]
        o_ref[dst:dst + NOPE_DIM, :] = (qt[src:src + NOPE_DIM] * Q_PRESCALE).astype(o_ref.dtype)
        o_ref[dst + NOPE_DIM:dst + NOPE_DIM + HALF_ROPE, :] = (
            (t1 * cos_t - t2 * sin_t) * Q_PRESCALE).astype(o_ref.dtype)
        o_ref[dst + NOPE_DIM + HALF_ROPE:dst + QK_DIM, :] = (
            (t2 * cos_t + t1 * sin_t) * Q_PRESCALE).astype(o_ref.dtype)
        o_ref[dst + QK_DIM:dst + QK_PAD, :] = zeros


def _q_up(qn, pos_row, wuq_t, invf_col, batch, seq, tm=512):
    m, r = qn.shape
    spb = seq // tm
    const = lambda i: (0, 0)
    return pl.pallas_call(
        _q_up_kernel,
        grid=(m // tm,),
        in_specs=[pl.BlockSpec((tm, r), lambda i: (i, 0)),
                  pl.BlockSpec((1, tm), lambda i: (0, i)),
                  pl.BlockSpec(wuq_t.shape, const),
                  pl.BlockSpec((HALF_ROPE, 1), const)],
        out_specs=pl.BlockSpec((None, A_HEADS * QK_PAD, tm), lambda i: (i // spb, 0, i % spb)),
        out_shape=jax.ShapeDtypeStruct((batch, A_HEADS * QK_PAD, seq), BF16),
        compiler_params=_params(("parallel",), 48),
        name="q_up",
    )(qn, pos_row, wuq_t, invf_col)


def _kv_up_kernel(kvn_ref, kr_ref, pos_ref, wk_ref, wvt_ref, invf_ref, sgn_ref, k_ref, vt_ref):
    kvn = kvn_ref[...]
    kn = jnp.dot(kvn, wk_ref[...], preferred_element_type=F32)
    vt_ref[...] = lax.dot_general(wvt_ref[...], kvn, _NT,
                                  preferred_element_type=F32).astype(vt_ref.dtype)
    ang = pos_ref[...].astype(F32) * invf_ref[...]
    kr = kr_ref[...]
    rope = (kr[:, :LANES] * jnp.cos(ang)
            + kr[:, LANES:] * (jnp.sin(ang) * sgn_ref[...])).astype(k_ref.dtype)
    for h in range(A_HEADS):
        lo = h * QK_PAD
        k_ref[:, lo:lo + NOPE_DIM] = kn[:, h * NOPE_DIM:(h + 1) * NOPE_DIM].astype(k_ref.dtype)
        k_ref[:, lo + NOPE_DIM:lo + QK_PAD] = rope


def _kv_up(kvn, kr, pos_col, wk, wv_t, invf_row, sgn_row, batch, seq, tm=512):
    m = kvn.shape[0]
    spb = seq // tm
    const = lambda i: (0, 0)
    return pl.pallas_call(
        _kv_up_kernel,
        grid=(m // tm,),
        in_specs=[pl.BlockSpec((tm, KV_LORA), lambda i: (i, 0)),
                  pl.BlockSpec((tm, 2 * LANES), lambda i: (i, 0)),
                  pl.BlockSpec((tm, 1), lambda i: (i, 0)),
                  pl.BlockSpec(wk.shape, const),
                  pl.BlockSpec(wv_t.shape, const),
                  pl.BlockSpec((1, LANES), const),
                  pl.BlockSpec((1, LANES), const)],
        out_specs=[pl.BlockSpec((tm, A_HEADS * QK_PAD), lambda i: (i, 0)),
                   pl.BlockSpec((None, A_WIDTH, tm), lambda i: (i // spb, 0, i % spb))],
        out_shape=[jax.ShapeDtypeStruct((m, A_HEADS * QK_PAD), BF16),
                   jax.ShapeDtypeStruct((batch, A_WIDTH, seq), BF16)],
        compiler_params=_params(("parallel",), 48),
        name="kv_up",
    )(kvn, kr, pos_col, wk, wv_t, invf_row, sgn_row)


def _attn_kernel(qt_ref, k_ref, vt_ref, o_ref, sa_ref, sb_ref, *, tq, tk):
    seq = k_ref.shape[0]
    n_tiles = qt_ref.shape[1] // tq
    n_chunks = seq // tk
    s_bufs = (sa_ref, sb_ref)

    def scores_chunk(t, c, m8):
        rows = slice(c * tk, (c + 1) * tk)
        st = jnp.dot(k_ref[rows, :], qt_ref[:, t * tq:(t + 1) * tq],
                     preferred_element_type=F32)
        s_bufs[t % 2][rows, :] = st
        return jnp.maximum(m8, jnp.max(st.reshape(tk // SUBLANES, SUBLANES, tq), axis=0))

    def probs_chunk(t, c, m, l8, acc):
        rows = slice(c * tk, (c + 1) * tk)
        p = jnp.exp2(s_bufs[t % 2][rows, :] - m)
        l8 = l8 + jnp.sum(p.reshape(tk // SUBLANES, SUBLANES, tq), axis=0)
        acc = acc + jnp.dot(vt_ref[:, rows], p.astype(BF16), preferred_element_type=F32)
        return l8, acc

    neg_inf = jnp.full((SUBLANES, tq), -jnp.inf, F32)
    m8_next = neg_inf
    for c in range(n_chunks):
        m8_next = scores_chunk(0, c, m8_next)
    for t in range(n_tiles):
        m = jnp.max(m8_next, axis=0, keepdims=True)
        m8_next = neg_inf
        l8 = jnp.zeros((SUBLANES, tq), F32)
        acc = jnp.zeros((V_DIM, tq), F32)
        for c in range(n_chunks):
            l8, acc = probs_chunk(t, c, m, l8, acc)
            if t + 1 < n_tiles:
                m8_next = scores_chunk(t + 1, c, m8_next)
        l = jnp.sum(l8, axis=0, keepdims=True)
        o_ref[t * tq:(t + 1) * tq, :] = (acc / l).T.astype(o_ref.dtype)


def _attention(qt, k, vt, batch, seq, tq=512, tk=512, tiles_per_step=4):
    rows = tq * tiles_per_step
    nr = seq // rows
    return pl.pallas_call(
        functools.partial(_attn_kernel, tq=tq, tk=tk),
        grid=(batch, A_HEADS, nr),
        in_specs=[pl.BlockSpec((None, QK_PAD, rows), lambda b, h, i: (b, h, i)),
                  pl.BlockSpec((seq, QK_PAD), lambda b, h, i: (b, h)),
                  pl.BlockSpec((None, V_DIM, seq), lambda b, h, i: (b, h, 0))],
        out_specs=pl.BlockSpec((rows, V_DIM), lambda b, h, i: (b * nr + i, h)),
        out_shape=jax.ShapeDtypeStruct((batch * seq, A_WIDTH), BF16),
        scratch_shapes=[pltpu.VMEM((seq, tq), F32), pltpu.VMEM((seq, tq), F32)],
        compiler_params=_params(("parallel", "parallel", "parallel"), 48),
        name="attention",
    )(qt, k, vt)


def _gating_kernel(uv_ref, lng_ref, lnb_ref, ws_ref, bs_ref, og_ref, o_ref, go_ref):
    v = uv_ref[:, G_WIDTH:].astype(F32)
    mu = jnp.mean(v, axis=-1, keepdims=True)
    vc = v - mu
    vln = (vc * lax.rsqrt(jnp.mean(vc * vc, axis=-1, keepdims=True) + EPS) * lng_ref[...]
           + lnb_ref[...]).astype(BF16)
    rows = uv_ref.shape[0]
    for g in range(G_HEADS):
        cols = slice(g * G_HEAD_DIM, (g + 1) * G_HEAD_DIM)
        w = ws_ref[g]
        b = bs_ref[g]
        for c in range(rows // CHUNK):
            rs = slice(c * CHUNK, (c + 1) * CHUNK)
            s = jnp.dot(w, vln[rs, cols], preferred_element_type=F32) + b
            go_ref[rs, cols] = uv_ref[rs, cols].astype(F32) * s
    o_ref[...] = _rms(go_ref[...], og_ref[...]).astype(o_ref.dtype)


def _gating(uv, ln_g, ln_b, ws, bs, out_g, tm=512):
    m = uv.shape[0]
    const2 = lambda i: (0, 0)
    const3 = lambda i: (0, 0, 0)
    return pl.pallas_call(
        _gating_kernel,
        grid=(m // tm,),
        in_specs=[pl.BlockSpec((tm, 2 * G_WIDTH), lambda i: (i, 0)),
                  pl.BlockSpec((1, G_WIDTH), const2),
                  pl.BlockSpec((1, G_WIDTH), const2),
                  pl.BlockSpec(ws.shape, const3),
                  pl.BlockSpec(bs.shape, const3),
                  pl.BlockSpec((1, G_WIDTH), const2)],
        out_specs=pl.BlockSpec((tm, G_WIDTH), lambda i: (i, 0)),
        out_shape=jax.ShapeDtypeStruct((m, G_WIDTH), BF16),
        scratch_shapes=[pltpu.VMEM((tm, G_WIDTH), F32)],
        compiler_params=_params(("parallel",), 48),
        name="gating",
    )(uv, ln_g, ln_b, ws, bs, out_g)


def _out_proj_kernel(a_ref, gn_ref, ag_ref, w1_ref, w2_ref, o_ref, an_ref):
    @pl.when(pl.program_id(1) == 0)
    def _():
        an_ref[...] = _rms(a_ref[...].astype(F32), ag_ref[...]).astype(an_ref.dtype)

    acc = jnp.dot(an_ref[...], w1_ref[...].astype(BF16), preferred_element_type=F32)
    acc = acc + jnp.dot(gn_ref[...], w2_ref[...].astype(BF16), preferred_element_type=F32)
    o_ref[...] = acc.astype(o_ref.dtype)


def _out_proj(a_out, g_n, a_gain, w_out, tm=1024, tn=512):
    m, ka = a_out.shape
    kg = g_n.shape[1]
    n = w_out.shape[1]
    return pl.pallas_call(
        _out_proj_kernel,
        grid=(m // tm, n // tn),
        in_specs=[pl.BlockSpec((tm, ka), lambda i, j: (i, 0)),
                  pl.BlockSpec((tm, kg), lambda i, j: (i, 0)),
                  pl.BlockSpec((1, ka), lambda i, j: (0, 0)),
                  pl.BlockSpec((ka, tn), lambda i, j: (0, j)),
                  pl.BlockSpec((kg, tn), lambda i, j: (ka // kg, j))],
        out_specs=pl.BlockSpec((tm, tn), lambda i, j: (i, j)),
        out_shape=jax.ShapeDtypeStruct((m, n), BF16),
        scratch_shapes=[pltpu.VMEM((tm, ka), BF16)],
        compiler_params=_params(("parallel", "arbitrary"), 52),
        name="out_proj",
    )(a_out, g_n, a_gain, w_out, w_out)


def _gate_up_kernel(a_ref, wg_ref, wu_ref, o_ref):
    a = a_ref[...]
    g = jnp.dot(a, wg_ref[...].astype(BF16), preferred_element_type=F32)
    u = jnp.dot(a, wu_ref[...].astype(BF16), preferred_element_type=F32)
    o_ref[...] = (jax.nn.silu(g) * u).astype(o_ref.dtype)


def _gate_up(a, wg, wu, tm=1024, tn=256):
    m, k = a.shape
    n = wg.shape[1]
    wspec = pl.BlockSpec((k, tn), lambda i, j: (0, j))
    return pl.pallas_call(
        _gate_up_kernel,
        grid=(m // tm, n // tn),
        in_specs=[pl.BlockSpec((tm, k), lambda i, j: (i, 0)), wspec, wspec],
        out_specs=pl.BlockSpec((tm, tn), lambda i, j: (i, j)),
        out_shape=jax.ShapeDtypeStruct((m, n), BF16),
        compiler_params=_params(("parallel", "parallel"), 52),
        name="gate_up",
    )(a, wg, wu)


def _down_proj_kernel(a_ref, w_ref, o_ref):
    o_ref[...] = jnp.dot(a_ref[...], w_ref[...], preferred_element_type=F32).astype(o_ref.dtype)


def _down_proj(a, w, tm=512, tn=512):
    m, k = a.shape
    n = w.shape[1]
    return pl.pallas_call(
        _down_proj_kernel,
        grid=(m // tm, n // tn),
        in_specs=[pl.BlockSpec((tm, k), lambda i, j: (i, 0)),
                  pl.BlockSpec((k, tn), lambda i, j: (0, j))],
        out_specs=pl.BlockSpec((tm, tn), lambda i, j: (i, j)),
        out_shape=jax.ShapeDtypeStruct((m, n), BF16),
        compiler_params=_params(("parallel", "parallel"), 56),
        name="down_proj",
    )(a, w)


def kernel(x, positions, pre_mix_norm, w_in, q_norm, kv_norm, w_uq, w_ukv, v_ln_gain, v_ln_bias,
           w_spatial, b_spatial, attn_out_norm, gmlp_out_norm, w_out, post_mix_norm, pre_ffn_norm,
           w_gate, w_up, w_down, post_ffn_norm):
    batch, seq, d = x.shape
    tokens = batch * seq
    depth = w_in.shape[0]

    inv_freq = 1.0 / (ROPE_THETA ** (jnp.arange(0, ROPE_DIM, 2, dtype=F32) / ROPE_DIM))
    zeros_half = jnp.zeros((LANES - ROPE_DIM,), F32)
    invf_row = jnp.concatenate([inv_freq, inv_freq, zeros_half])[None, :]
    sgn_row = jnp.concatenate([-jnp.ones((HALF_ROPE,), F32), jnp.ones((HALF_ROPE,), F32),
                               zeros_half])[None, :]
    invf_col = inv_freq[:, None]
    pos_col = positions.reshape(tokens, 1)
    pos_row = positions.reshape(1, tokens)

    xf = x.reshape(tokens, d)
    for l in range(depth):
        wi = w_in[l]
        o_kv, o_kr, o_u = Q_LORA, Q_LORA + KV_LORA, Q_LORA + KV_LORA + ROPE_DIM
        w_kr = wi[:, o_kr:o_u]
        pad = jnp.zeros((d, LANES - ROPE_DIM), F32)
        w_qkv = jnp.concatenate(
            [wi[:, :o_kr], w_kr, pad, w_kr[:, HALF_ROPE:], w_kr[:, :HALF_ROPE], pad],
            axis=1).astype(BF16)
        w_uv = wi[:, o_u:].astype(BF16)
        wuq_t = w_uq[l].T.astype(BF16)
        wkv3 = w_ukv[l].reshape(KV_LORA, A_HEADS, NOPE_DIM + V_DIM)
        wk = wkv3[:, :, :NOPE_DIM].reshape(KV_LORA, A_HEADS * NOPE_DIM).astype(BF16)
        wv_t = wkv3[:, :, NOPE_DIM:].reshape(KV_LORA, A_WIDTH).T.astype(BF16)
        ws = w_spatial[l].astype(BF16)
        bs = b_spatial[l][:, :, None]
        w_d = w_down[l].astype(BF16)
        pre_g = pre_mix_norm[l][None, :]

        qn, kvn, kr = _qkv_down(xf, pre_g, w_qkv, q_norm[l][None, :], kv_norm[l][None, :])
        qt = _q_up(qn, pos_row, wuq_t, invf_col, batch, seq)
        k, vt = _kv_up(kvn, kr, pos_col, wk, wv_t, invf_row, sgn_row, batch, seq)
        a_out = _attention(qt, k, vt, batch, seq)

        uv = _uv_proj(xf, pre_g, w_uv)
        g_n = _gating(uv, v_ln_gain[l][None, :], v_ln_bias[l][None, :], ws, bs,
                      gmlp_out_norm[l][None, :])

        mix = _out_proj(a_out, g_n, attn_out_norm[l][None, :], w_out[l])
        h, hn = _post_mix(xf, mix, post_mix_norm[l][None, :], pre_ffn_norm[l][None, :])

        act = _gate_up(hn, w_gate[l], w_up[l])
        ffn = _down_proj(act, w_d)
        xf = _post_ffn(h, ffn, post_ffn_norm[l][None, :])

    return xf.reshape(batch, seq, d)
```

```python
import functools
import math

import jax
import jax.numpy as jnp
from jax import lax
from jax.experimental import pallas as pl
from jax.experimental.pallas import tpu as pltpu

F32 = jnp.float32
BF16 = jnp.bfloat16

D_MODEL = 4096
A_HEADS = 16
NOPE_DIM = 128
ROPE_DIM = 64
HALF_ROPE = ROPE_DIM // 2
QK_DIM = NOPE_DIM + ROPE_DIM
V_DIM = 128
A_WIDTH = A_HEADS * V_DIM
Q_LORA = 1024
KV_LORA = 512
ROPE_THETA = 10000.0
G_WIDTH = D_MODEL - A_WIDTH
G_HEADS = 16
G_HEAD_DIM = G_WIDTH // G_HEADS
CHUNK = 128
EPS = 1e-6

LANES = 128
SUBLANES = 8
QK_PAD = 2 * LANES
MIB = 1024 * 1024

QKV_COLS = Q_LORA + KV_LORA + 2 * LANES
UV_COL0 = 2048

Q_PRESCALE = (1.0 / math.sqrt(QK_DIM)) * math.log2(math.e)

_NT = (((1,), (1,)), ((), ()))


def _params(semantics, vmem_mib):
    return pltpu.CompilerParams(dimension_semantics=semantics,
                                vmem_limit_bytes=int(vmem_mib * MIB))


def _rms(x, g):
    return x * lax.rsqrt(jnp.mean(x * x, axis=-1, keepdims=True) + EPS) * g


def _post_mix_kernel(x_ref, mix_ref, g1_ref, g2_ref, h_ref, hn_ref):
    h = x_ref[...] + _rms(mix_ref[...].astype(F32), g1_ref[...])
    h_ref[...] = h
    hn_ref[...] = _rms(h, g2_ref[...]).astype(hn_ref.dtype)


def _post_mix(x, mix, g1, g2, tm=256):
    m, d = x.shape
    row = pl.BlockSpec((tm, d), lambda i: (i, 0))
    gain = pl.BlockSpec((1, d), lambda i: (0, 0))
    return pl.pallas_call(
        _post_mix_kernel,
        grid=(m // tm,),
        in_specs=[row, row, gain, gain],
        out_specs=[row, row],
        out_shape=[jax.ShapeDtypeStruct((m, d), F32),
                   jax.ShapeDtypeStruct((m, d), BF16)],
        compiler_params=_params(("parallel",), 48),
        name="post_mix",
    )(x, mix, g1, g2)


def _post_ffn_kernel(h_ref, f_ref, g_ref, o_ref):
    o_ref[...] = h_ref[...] + _rms(f_ref[...].astype(F32), g_ref[...])


def _post_ffn(h, ffn, g, tm=256):
    m, d = h.shape
    row = pl.BlockSpec((tm, d), lambda i: (i, 0))
    return pl.pallas_call(
        _post_ffn_kernel,
        grid=(m // tm,),
        in_specs=[row, row, pl.BlockSpec((1, d), lambda i: (0, 0))],
        out_specs=row,
        out_shape=jax.ShapeDtypeStruct((m, d), F32),
        compiler_params=_params(("parallel",), 48),
        name="post_ffn",
    )(h, ffn, g)


def _qkv_down_kernel(x_ref, pg_ref, w_ref, qg_ref, kg_ref, qn_ref, kvn_ref, kr_ref, xn_ref):
    xn = _rms(x_ref[...], pg_ref[...]).astype(BF16)
    xn_ref[...] = xn
    acc = jnp.dot(xn, w_ref[...], preferred_element_type=F32)
    qn_ref[...] = _rms(acc[:, :Q_LORA], qg_ref[...]).astype(qn_ref.dtype)
    kvn_ref[...] = _rms(acc[:, Q_LORA:Q_LORA + KV_LORA], kg_ref[...]).astype(kvn_ref.dtype)
    kr_ref[...] = acc[:, Q_LORA + KV_LORA:]


def _qkv_down(x, pre_g, w_all, q_g, kv_g, tm=256):
    m, d = x.shape
    n_kr = QKV_COLS - Q_LORA - KV_LORA
    const = lambda i: (0, 0)
    row = lambda i: (i, 0)
    return pl.pallas_call(
        _qkv_down_kernel,
        grid=(m // tm,),
        in_specs=[pl.BlockSpec((tm, d), row),
                  pl.BlockSpec((1, d), const),
                  pl.BlockSpec((d, QKV_COLS), const),
                  pl.BlockSpec((1, Q_LORA), const),
                  pl.BlockSpec((1, KV_LORA), const)],
        out_specs=[pl.BlockSpec((tm, Q_LORA), row),
                   pl.BlockSpec((tm, KV_LORA), row),
                   pl.BlockSpec((tm, n_kr), row),
                   pl.BlockSpec((tm, d), row)],
        out_shape=[jax.ShapeDtypeStruct((m, Q_LORA), BF16),
                   jax.ShapeDtypeStruct((m, KV_LORA), BF16),
                   jax.ShapeDtypeStruct((m, n_kr), F32),
                   jax.ShapeDtypeStruct((m, d), BF16)],
        compiler_params=_params(("parallel",), 56),
        name="qkv_down",
    )(x, pre_g, w_all, q_g, kv_g)


def _uv_proj_kernel(xn_ref, w_ref, o_ref):
    acc = jnp.dot(xn_ref[...], w_ref[...], preferred_element_type=F32)
    o_ref[...] = jax.nn.gelu(acc).astype(o_ref.dtype)


def _uv_proj(xn, w_all, tm=1024, tn=512):
    m, d = xn.shape
    n = 2 * G_WIDTH
    j0 = UV_COL0 // tn
    return pl.pallas_call(
        _uv_proj_kernel,
        grid=(m // tm, n // tn),
        in_specs=[pl.BlockSpec((tm, d), lambda i, j: (i, 0)),
                  pl.BlockSpec((d, tn), lambda i, j: (0, j0 + j))],
        out_specs=pl.BlockSpec((tm, tn), lambda i, j: (i, j)),
        out_shape=jax.ShapeDtypeStruct((m, n), BF16),
        compiler_params=_params(("parallel", "parallel"), 48),
        name="uv_proj",
    )(xn, w_all)


def _q_up_kernel(qn_ref, pos_ref, wt_ref, invf_ref, o_ref):
    qt = lax.dot_general(wt_ref[...], qn_ref[...], _NT, preferred_element_type=F32)
    ang = invf_ref[...] * pos_ref[...].astype(F32)
    cos_t = jnp.cos(ang)
    sin_t = jnp.sin(ang)
    zeros = jnp.zeros((QK_PAD - QK_DIM, qt.shape[1]), o_ref.dtype)
    for h in range(A_HEADS):
        src = h * QK_DIM
        dst = h * QK_PAD
        t1 = qt[src + NOPE_DIM:src + NOPE_DIM + HALF_ROPE]
        t2 = qt[src + NOPE_DIM + HALF_ROPE:src + QK_DIM]
        o_ref[dst:dst + NOPE_DIM, :] = (qt[src:src + NOPE_DIM] * Q_PRESCALE).astype(o_ref.dtype)
        o_ref[dst + NOPE_DIM:dst + NOPE_DIM + HALF_ROPE, :] = (
            (t1 * cos_t - t2 * sin_t) * Q_PRESCALE).astype(o_ref.dtype)
        o_ref[dst + NOPE_DIM + HALF_ROPE:dst + QK_DIM, :] = (
            (t2 * cos_t + t1 * sin_t) * Q_PRESCALE).astype(o_ref.dtype)
        o_ref[dst + QK_DIM:dst + QK_PAD, :] = zeros


def _q_up(qn, pos_row, wuq_t, invf_col, batch, seq, tm=512):
    m, r = qn.shape
    spb = seq // tm
    const = lambda i: (0, 0)
    return pl.pallas_call(
        _q_up_kernel,
        grid=(m // tm,),
        in_specs=[pl.BlockSpec((tm, r), lambda i: (i, 0)),
                  pl.BlockSpec((1, tm), lambda i: (0, i)),
                  pl.BlockSpec(wuq_t.shape, const),
                  pl.BlockSpec((HALF_ROPE, 1), const)],
        out_specs=pl.BlockSpec((None, A_HEADS * QK_PAD, tm), lambda i: (i // spb, 0, i % spb)),
        out_shape=jax.ShapeDtypeStruct((batch, A_HEADS * QK_PAD, seq), BF16),
        compiler_params=_params(("parallel",), 48),
        name="q_up",
    )(qn, pos_row, wuq_t, invf_col)


def _kv_up_kernel(kvn_ref, kr_ref, pos_ref, wk_ref, wvt_ref, invf_ref, sgn_ref, k_ref, vt_ref):
    kvn = kvn_ref[...]
    kn = jnp.dot(kvn, wk_ref[...], preferred_element_type=F32)
    vt_ref[...] = lax.dot_general(wvt_ref[...], kvn, _NT,
                                  preferred_element_type=F32).astype(vt_ref.dtype)
    ang = pos_ref[...].astype(F32) * invf_ref[...]
    kr = kr_ref[...]
    rope = (kr[:, :LANES] * jnp.cos(ang)
            + kr[:, LANES:] * (jnp.sin(ang) * sgn_ref[...])).astype(k_ref.dtype)
    for h in range(A_HEADS):
        lo = h * QK_PAD
        k_ref[:, lo:lo + NOPE_DIM] = kn[:, h * NOPE_DIM:(h + 1) * NOPE_DIM].astype(k_ref.dtype)
        k_ref[:, lo + NOPE_DIM:lo + QK_PAD] = rope


def _kv_up(kvn, kr, pos_col, wk, wv_t, invf_row, sgn_row, batch, seq, tm=512):
    m = kvn.shape[0]
    spb = seq // tm
    const = lambda i: (0, 0)
    return pl.pallas_call(
        _kv_up_kernel,
        grid=(m // tm,),
        in_specs=[pl.BlockSpec((tm, KV_LORA), lambda i: (i, 0)),
                  pl.BlockSpec((tm, 2 * LANES), lambda i: (i, 0)),
                  pl.BlockSpec((tm, 1), lambda i: (i, 0)),
                  pl.BlockSpec(wk.shape, const),
                  pl.BlockSpec(wv_t.shape, const),
                  pl.BlockSpec((1, LANES), const),
                  pl.BlockSpec((1, LANES), const)],
        out_specs=[pl.BlockSpec((tm, A_HEADS * QK_PAD), lambda i: (i, 0)),
                   pl.BlockSpec((None, A_WIDTH, tm), lambda i: (i // spb, 0, i % spb))],
        out_shape=[jax.ShapeDtypeStruct((m, A_HEADS * QK_PAD), BF16),
                   jax.ShapeDtypeStruct((batch, A_WIDTH, seq), BF16)],
        compiler_params=_params(("parallel",), 48),
        name="kv_up",
    )(kvn, kr, pos_col, wk, wv_t, invf_row, sgn_row)


def _attn_kernel(qt_ref, k_ref, vt_ref, o_ref, sa_ref, sb_ref, *, tq, tk):
    seq = k_ref.shape[0]
    n_tiles = qt_ref.shape[1] // tq
    n_chunks = seq // tk
    s_bufs = (sa_ref, sb_ref)

    def scores_chunk(t, c, m8):
        rows = slice(c * tk, (c + 1) * tk)
        st = jnp.dot(k_ref[rows, :], qt_ref[:, t * tq:(t + 1) * tq],
                     preferred_element_type=F32)
        s_bufs[t % 2][rows, :] = st
        return jnp.maximum(m8, jnp.max(st.reshape(tk // SUBLANES, SUBLANES, tq), axis=0))

    def probs_chunk(t, c, m, l8, acc):
        rows = slice(c * tk, (c + 1) * tk)
        p = jnp.exp2(s_bufs[t % 2][rows, :] - m)
        l8 = l8 + jnp.sum(p.reshape(tk // SUBLANES, SUBLANES, tq), axis=0)
        acc = acc + jnp.dot(vt_ref[:, rows], p.astype(BF16), preferred_element_type=F32)
        return l8, acc

    neg_inf = jnp.full((SUBLANES, tq), -jnp.inf, F32)
    m8_next = neg_inf
    for c in range(n_chunks):
        m8_next = scores_chunk(0, c, m8_next)
    for t in range(n_tiles):
        m = jnp.max(m8_next, axis=0, keepdims=True)
        m8_next = neg_inf
        l8 = jnp.zeros((SUBLANES, tq), F32)
        acc = jnp.zeros((V_DIM, tq), F32)
        for c in range(n_chunks):
            l8, acc = probs_chunk(t, c, m, l8, acc)
            if t + 1 < n_tiles:
                m8_next = scores_chunk(t + 1, c, m8_next)
        l = jnp.sum(l8, axis=0, keepdims=True)
        o_ref[t * tq:(t + 1) * tq, :] = (acc / l).T.astype(o_ref.dtype)


def _attention(qt, k, vt, batch, seq, tq=512, tk=512, tiles_per_step=4):
    rows = tq * tiles_per_step
    nr = seq // rows
    return pl.pallas_call(
        functools.partial(_attn_kernel, tq=tq, tk=tk),
        grid=(batch, A_HEADS, nr),
        in_specs=[pl.BlockSpec((None, QK_PAD, rows), lambda b, h, i: (b, h, i)),
                  pl.BlockSpec((seq, QK_PAD), lambda b, h, i: (b, h)),
                  pl.BlockSpec((None, V_DIM, seq), lambda b, h, i: (b, h, 0))],
        out_specs=pl.BlockSpec((rows, V_DIM), lambda b, h, i: (b * nr + i, h)),
        out_shape=jax.ShapeDtypeStruct((batch * seq, A_WIDTH), BF16),
        scratch_shapes=[pltpu.VMEM((seq, tq), F32), pltpu.VMEM((seq, tq), F32)],
        compiler_params=_params(("parallel", "parallel", "parallel"), 48),
        name="attention",
    )(qt, k, vt)


def _gating_kernel(uv_ref, lng_ref, lnb_ref, ws_ref, bs_ref, og_ref, o_ref, go_ref):
    v = uv_ref[:, G_WIDTH:].astype(F32)
    mu = jnp.mean(v, axis=-1, keepdims=True)
    vc = v - mu
    vln = (vc * lax.rsqrt(jnp.mean(vc * vc, axis=-1, keepdims=True) + EPS) * lng_ref[...]
           + lnb_ref[...]).astype(BF16)
    rows = uv_ref.shape[0]
    for g in range(G_HEADS):
        cols = slice(g * G_HEAD_DIM, (g + 1) * G_HEAD_DIM)
        w = ws_ref[g]
        b = bs_ref[g]
        for c in range(rows // CHUNK):
            rs = slice(c * CHUNK, (c + 1) * CHUNK)
            s = jnp.dot(w, vln[rs, cols], preferred_element_type=F32) + b
            go_ref[rs, cols] = uv_ref[rs, cols].astype(F32) * s
    o_ref[...] = _rms(go_ref[...], og_ref[...]).astype(o_ref.dtype)


def _gating(uv, ln_g, ln_b, ws, bs, out_g, tm=512):
    m = uv.shape[0]
    const2 = lambda i: (0, 0)
    const3 = lambda i: (0, 0, 0)
    return pl.pallas_call(
        _gating_kernel,
        grid=(m // tm,),
        in_specs=[pl.BlockSpec((tm, 2 * G_WIDTH), lambda i: (i, 0)),
                  pl.BlockSpec((1, G_WIDTH), const2),
                  pl.BlockSpec((1, G_WIDTH), const2),
                  pl.BlockSpec(ws.shape, const3),
                  pl.BlockSpec(bs.shape, const3),
                  pl.BlockSpec((1, G_WIDTH), const2)],
        out_specs=pl.BlockSpec((tm, G_WIDTH), lambda i: (i, 0)),
        out_shape=jax.ShapeDtypeStruct((m, G_WIDTH), BF16),
        scratch_shapes=[pltpu.VMEM((tm, G_WIDTH), F32)],
        compiler_params=_params(("parallel",), 48),
        name="gating",
    )(uv, ln_g, ln_b, ws, bs, out_g)


def _out_proj_kernel(a_ref, gn_ref, ag_ref, w1_ref, w2_ref, o_ref, an_ref):
    @pl.when(pl.program_id(1) == 0)
    def _():
        an_ref[...] = _rms(a_ref[...].astype(F32), ag_ref[...]).astype(an_ref.dtype)

    acc = jnp.dot(an_ref[...], w1_ref[...].astype(BF16), preferred_element_type=F32)
    acc = acc + jnp.dot(gn_ref[...], w2_ref[...].astype(BF16), preferred_element_type=F32)
    o_ref[...] = acc.astype(o_ref.dtype)


def _out_proj(a_out, g_n, a_gain, w_out, tm=1024, tn=512):
    m, ka = a_out.shape
    kg = g_n.shape[1]
    n = w_out.shape[1]
    return pl.pallas_call(
        _out_proj_kernel,
        grid=(m // tm, n // tn),
        in_specs=[pl.BlockSpec((tm, ka), lambda i, j: (i, 0)),
                  pl.BlockSpec((tm, kg), lambda i, j: (i, 0)),
                  pl.BlockSpec((1, ka), lambda i, j: (0, 0)),
                  pl.BlockSpec((ka, tn), lambda i, j: (0, j)),
                  pl.BlockSpec((kg, tn), lambda i, j: (ka // kg, j))],
        out_specs=pl.BlockSpec((tm, tn), lambda i, j: (i, j)),
        out_shape=jax.ShapeDtypeStruct((m, n), BF16),
        scratch_shapes=[pltpu.VMEM((tm, ka), BF16)],
        compiler_params=_params(("parallel", "arbitrary"), 52),
        name="out_proj",
    )(a_out, g_n, a_gain, w_out, w_out)


def _gate_up_kernel(a_ref, wg_ref, wu_ref, o_ref):
    a = a_ref[...]
    g = jnp.dot(a, wg_ref[...].astype(BF16), preferred_element_type=F32)
    u = jnp.dot(a, wu_ref[...].astype(BF16), preferred_element_type=F32)
    o_ref[...] = (jax.nn.silu(g) * u).astype(o_ref.dtype)


def _gate_up(a, wg, wu, tm=1024, tn=256):
    m, k = a.shape
    n = wg.shape[1]
    wspec = pl.BlockSpec((k, tn), lambda i, j: (0, j))
    return pl.pallas_call(
        _gate_up_kernel,
        grid=(m // tm, n // tn),
        in_specs=[pl.BlockSpec((tm, k), lambda i, j: (i, 0)), wspec, wspec],
        out_specs=pl.BlockSpec((tm, tn), lambda i, j: (i, j)),
        out_shape=jax.ShapeDtypeStruct((m, n), BF16),
        compiler_params=_params(("parallel", "parallel"), 52),
        name="gate_up",
    )(a, wg, wu)


def _down_proj_kernel(a_ref, w_ref, o_ref):
    o_ref[...] = jnp.dot(a_ref[...], w_ref[...], preferred_element_type=F32).astype(o_ref.dtype)


def _down_proj(a, w, tm=512, tn=512):
    m, k = a.shape
    n = w.shape[1]
    return pl.pallas_call(
        _down_proj_kernel,
        grid=(m // tm, n // tn),
        in_specs=[pl.BlockSpec((tm, k), lambda i, j: (i, 0)),
                  pl.BlockSpec((k, tn), lambda i, j: (0, j))],
        out_specs=pl.BlockSpec((tm, tn), lambda i, j: (i, j)),
        out_shape=jax.ShapeDtypeStruct((m, n), BF16),
        compiler_params=_params(("parallel", "parallel"), 56),
        name="down_proj",
    )(a, w)


def kernel(x, positions, pre_mix_norm, w_in, q_norm, kv_norm, w_uq, w_ukv, v_ln_gain, v_ln_bias,
           w_spatial, b_spatial, attn_out_norm, gmlp_out_norm, w_out, post_mix_norm, pre_ffn_norm,
           w_gate, w_up, w_down, post_ffn_norm):
    batch, seq, d = x.shape
    tokens = batch * seq
    depth = w_in.shape[0]

    inv_freq = 1.0 / (ROPE_THETA ** (jnp.arange(0, ROPE_DIM, 2, dtype=F32) / ROPE_DIM))
    zeros_half = jnp.zeros((LANES - ROPE_DIM,), F32)
    invf_row = jnp.concatenate([inv_freq, inv_freq, zeros_half])[None, :]
    sgn_row = jnp.concatenate([-jnp.ones((HALF_ROPE,), F32), jnp.ones((HALF_ROPE,), F32),
                               zeros_half])[None, :]
    invf_col = inv_freq[:, None]
    pos_col = positions.reshape(tokens, 1)
    pos_row = positions.reshape(1, tokens)

    xf = x.reshape(tokens, d)
    for l in range(depth):
        wi = w_in[l]
        o_kv, o_kr, o_u = Q_LORA, Q_LORA + KV_LORA, Q_LORA + KV_LORA + ROPE_DIM
        w_kr = wi[:, o_kr:o_u]
        pad = jnp.zeros((d, LANES - ROPE_DIM), F32)
        w_all = jnp.concatenate(
            [wi[:, :o_kr], w_kr, pad, w_kr[:, HALF_ROPE:], w_kr[:, :HALF_ROPE], pad,
             jnp.zeros((d, UV_COL0 - QKV_COLS), F32), wi[:, o_u:]],
            axis=1).astype(BF16)
        wuq_t = w_uq[l].T.astype(BF16)
        wkv3 = w_ukv[l].reshape(KV_LORA, A_HEADS, NOPE_DIM + V_DIM)
        wk = wkv3[:, :, :NOPE_DIM].reshape(KV_LORA, A_HEADS * NOPE_DIM).astype(BF16)
        wv_t = wkv3[:, :, NOPE_DIM:].reshape(KV_LORA, A_WIDTH).T.astype(BF16)
        ws = w_spatial[l].astype(BF16)
        bs = b_spatial[l][:, :, None]
        w_d = w_down[l].astype(BF16)
        pre_g = pre_mix_norm[l][None, :]

        qn, kvn, kr, xn = _qkv_down(xf, pre_g, w_all, q_norm[l][None, :], kv_norm[l][None, :])
        qt = _q_up(qn, pos_row, wuq_t, invf_col, batch, seq)
        k, vt = _kv_up(kvn, kr, pos_col, wk, wv_t, invf_row, sgn_row, batch, seq)
        a_out = _attention(qt, k, vt, batch, seq)

        uv = _uv_proj(xn, w_all)
        g_n = _gating(uv, v_ln_gain[l][None, :], v_ln_bias[l][None, :], ws, bs,
                      gmlp_out_norm[l][None, :])

        mix = _out_proj(a_out, g_n, attn_out_norm[l][None, :], w_out[l])
        h, hn = _post_mix(xf, mix, post_mix_norm[l][None, :], pre_ffn_norm[l][None, :])

        act = _gate_up(hn, w_gate[l], w_up[l])
        ffn = _down_proj(act, w_d)
        xf = _post_ffn(h, ffn, post_ffn_norm[l][None, :])

    return xf.reshape(batch, seq, d)
```

```python
import functools
import math

import jax
import jax.numpy as jnp
from jax import lax
from jax.experimental import pallas as pl
from jax.experimental.pallas import tpu as pltpu

F32 = jnp.float32
BF16 = jnp.bfloat16

D_MODEL = 4096
A_HEADS = 16
NOPE_DIM = 128
ROPE_DIM = 64
HALF_ROPE = ROPE_DIM // 2
QK_DIM = NOPE_DIM + ROPE_DIM
V_DIM = 128
A_WIDTH = A_HEADS * V_DIM
Q_LORA = 1024
KV_LORA = 512
ROPE_THETA = 10000.0
G_WIDTH = D_MODEL - A_WIDTH
G_HEADS = 16
G_HEAD_DIM = G_WIDTH // G_HEADS
CHUNK = 128
EPS = 1e-6

LANES = 128
SUBLANES = 8
QK_PAD = 2 * LANES
MIB = 1024 * 1024

QKV_COLS = Q_LORA + KV_LORA + 2 * LANES
UV_COL0 = 2048

Q_PRESCALE = (1.0 / math.sqrt(QK_DIM)) * math.log2(math.e)

_NT = (((1,), (1,)), ((), ()))


def _params(semantics, vmem_mib):
    return pltpu.CompilerParams(dimension_semantics=semantics,
                                vmem_limit_bytes=int(vmem_mib * MIB))


def _rms(x, g):
    return x * lax.rsqrt(jnp.mean(x * x, axis=-1, keepdims=True) + EPS) * g


def _post_mix_kernel(x_ref, mix_ref, g1_ref, g2_ref, h_ref, hn_ref):
    h = x_ref[...] + _rms(mix_ref[...].astype(F32), g1_ref[...])
    h_ref[...] = h
    hn_ref[...] = _rms(h, g2_ref[...]).astype(hn_ref.dtype)


def _post_mix(x, mix, g1, g2, tm=256):
    m, d = x.shape
    row = pl.BlockSpec((tm, d), lambda i: (i, 0))
    gain = pl.BlockSpec((1, d), lambda i: (0, 0))
    return pl.pallas_call(
        _post_mix_kernel,
        grid=(m // tm,),
        in_specs=[row, row, gain, gain],
        out_specs=[row, row],
        out_shape=[jax.ShapeDtypeStruct((m, d), F32),
                   jax.ShapeDtypeStruct((m, d), BF16)],
        compiler_params=_params(("parallel",), 48),
        name="post_mix",
    )(x, mix, g1, g2)


def _post_ffn_kernel(h_ref, f_ref, g_ref, o_ref):
    o_ref[...] = h_ref[...] + _rms(f_ref[...].astype(F32), g_ref[...])


def _post_ffn(h, ffn, g, tm=256):
    m, d = h.shape
    row = pl.BlockSpec((tm, d), lambda i: (i, 0))
    return pl.pallas_call(
        _post_ffn_kernel,
        grid=(m // tm,),
        in_specs=[row, row, pl.BlockSpec((1, d), lambda i: (0, 0))],
        out_specs=row,
        out_shape=jax.ShapeDtypeStruct((m, d), F32),
        compiler_params=_params(("parallel",), 48),
        name="post_ffn",
    )(h, ffn, g)


def _qkv_down_kernel(x_ref, pg_ref, w_ref, qg_ref, kg_ref, qn_ref, kvn_ref, kr_ref, xn_ref):
    xn = _rms(x_ref[...], pg_ref[...]).astype(BF16)
    xn_ref[...] = xn
    acc = jnp.dot(xn, w_ref[...], preferred_element_type=F32)
    qn_ref[...] = _rms(acc[:, :Q_LORA], qg_ref[...]).astype(qn_ref.dtype)
    kvn_ref[...] = _rms(acc[:, Q_LORA:Q_LORA + KV_LORA], kg_ref[...]).astype(kvn_ref.dtype)
    kr_ref[...] = acc[:, Q_LORA + KV_LORA:]


def _qkv_down(x, pre_g, w_all, q_g, kv_g, tm=512):
    m, d = x.shape
    n_kr = QKV_COLS - Q_LORA - KV_LORA
    const = lambda i: (0, 0)
    row = lambda i: (i, 0)
    return pl.pallas_call(
        _qkv_down_kernel,
        grid=(m // tm,),
        in_specs=[pl.BlockSpec((tm, d), row),
                  pl.BlockSpec((1, d), const),
                  pl.BlockSpec((d, QKV_COLS), const, pipeline_mode=pl.Buffered(1)),
                  pl.BlockSpec((1, Q_LORA), const),
                  pl.BlockSpec((1, KV_LORA), const)],
        out_specs=[pl.BlockSpec((tm, Q_LORA), row),
                   pl.BlockSpec((tm, KV_LORA), row),
                   pl.BlockSpec((tm, n_kr), row),
                   pl.BlockSpec((tm, d), row)],
        out_shape=[jax.ShapeDtypeStruct((m, Q_LORA), BF16),
                   jax.ShapeDtypeStruct((m, KV_LORA), BF16),
                   jax.ShapeDtypeStruct((m, n_kr), F32),
                   jax.ShapeDtypeStruct((m, d), BF16)],
        compiler_params=_params(("parallel",), 56),
        name="qkv_down",
    )(x, pre_g, w_all, q_g, kv_g)


def _uv_proj_kernel(xn_ref, w_ref, o_ref):
    acc = jnp.dot(xn_ref[...], w_ref[...], preferred_element_type=F32)
    o_ref[...] = jax.nn.gelu(acc).astype(o_ref.dtype)


def _uv_proj(xn, w_all, tm=1024, tn=512):
    m, d = xn.shape
    n = 2 * G_WIDTH
    j0 = UV_COL0 // tn
    return pl.pallas_call(
        _uv_proj_kernel,
        grid=(m // tm, n // tn),
        in_specs=[pl.BlockSpec((tm, d), lambda i, j: (i, 0)),
                  pl.BlockSpec((d, tn), lambda i, j: (0, j0 + j))],
        out_specs=pl.BlockSpec((tm, tn), lambda i, j: (i, j)),
        out_shape=jax.ShapeDtypeStruct((m, n), BF16),
        compiler_params=_params(("parallel", "parallel"), 48),
        name="uv_proj",
    )(xn, w_all)


def _q_up_kernel(qn_ref, pos_ref, wt_ref, invf_ref, o_ref):
    qt = lax.dot_general(wt_ref[...], qn_ref[...], _NT, preferred_element_type=F32)
    ang = invf_ref[...] * pos_ref[...].astype(F32)
    cos_t = jnp.cos(ang)
    sin_t = jnp.sin(ang)
    zeros = jnp.zeros((QK_PAD - QK_DIM, qt.shape[1]), o_ref.dtype)
    for h in range(A_HEADS):
        src = h * QK_DIM
        dst = h * QK_PAD
        t1 = qt[src + NOPE_DIM:src + NOPE_DIM + HALF_ROPE]
        t2 = qt[src + NOPE_DIM + HALF_ROPE:src + QK_DIM]
        o_ref[dst:dst + NOPE_DIM, :] = (qt[src:src + NOPE_DIM] * Q_PRESCALE).astype(o_ref.dtype)
        o_ref[dst + NOPE_DIM:dst + NOPE_DIM + HALF_ROPE, :] = (
            (t1 * cos_t - t2 * sin_t) * Q_PRESCALE).astype(o_ref.dtype)
        o_ref[dst + NOPE_DIM + HALF_ROPE:dst + QK_DIM, :] = (
            (t2 * cos_t + t1 * sin_t) * Q_PRESCALE).astype(o_ref.dtype)
        o_ref[dst + QK_DIM:dst + QK_PAD, :] = zeros


def _q_up(qn, pos_row, wuq_t, invf_col, batch, seq, tm=512):
    m, r = qn.shape
    spb = seq // tm
    const = lambda i: (0, 0)
    return pl.pallas_call(
        _q_up_kernel,
        grid=(m // tm,),
        in_specs=[pl.BlockSpec((tm, r), lambda i: (i, 0)),
                  pl.BlockSpec((1, tm), lambda i: (0, i)),
                  pl.BlockSpec(wuq_t.shape, const),
                  pl.BlockSpec((HALF_ROPE, 1), const)],
        out_specs=pl.BlockSpec((None, A_HEADS * QK_PAD, tm), lambda i: (i // spb, 0, i % spb)),
        out_shape=jax.ShapeDtypeStruct((batch, A_HEADS * QK_PAD, seq), BF16),
        compiler_params=_params(("parallel",), 48),
        name="q_up",
    )(qn, pos_row, wuq_t, invf_col)


def _kv_up_kernel(kvn_ref, kr_ref, pos_ref, wk_ref, wvt_ref, invf_ref, sgn_ref, k_ref, vt_ref):
    kvn = kvn_ref[...]
    kn = jnp.dot(kvn, wk_ref[...], preferred_element_type=F32)
    vt_ref[...] = lax.dot_general(wvt_ref[...], kvn, _NT,
                                  preferred_element_type=F32).astype(vt_ref.dtype)
    ang = pos_ref[...].astype(F32) * invf_ref[...]
    kr = kr_ref[...]
    rope = (kr[:, :LANES] * jnp.cos(ang)
            + kr[:, LANES:] * (jnp.sin(ang) * sgn_ref[...])).astype(k_ref.dtype)
    for h in range(A_HEADS):
        lo = h * QK_PAD
        k_ref[:, lo:lo + NOPE_DIM] = kn[:, h * NOPE_DIM:(h + 1) * NOPE_DIM].astype(k_ref.dtype)
        k_ref[:, lo + NOPE_DIM:lo + QK_PAD] = rope


def _kv_up(kvn, kr, pos_col, wk, wv_t, invf_row, sgn_row, batch, seq, tm=512):
    m = kvn.shape[0]
    spb = seq // tm
    const = lambda i: (0, 0)
    return pl.pallas_call(
        _kv_up_kernel,
        grid=(m // tm,),
        in_specs=[pl.BlockSpec((tm, KV_LORA), lambda i: (i, 0)),
                  pl.BlockSpec((tm, 2 * LANES), lambda i: (i, 0)),
                  pl.BlockSpec((tm, 1), lambda i: (i, 0)),
                  pl.BlockSpec(wk.shape, const),
                  pl.BlockSpec(wv_t.shape, const),
                  pl.BlockSpec((1, LANES), const),
                  pl.BlockSpec((1, LANES), const)],
        out_specs=[pl.BlockSpec((tm, A_HEADS * QK_PAD), lambda i: (i, 0)),
                   pl.BlockSpec((None, A_WIDTH, tm), lambda i: (i // spb, 0, i % spb))],
        out_shape=[jax.ShapeDtypeStruct((m, A_HEADS * QK_PAD), BF16),
                   jax.ShapeDtypeStruct((batch, A_WIDTH, seq), BF16)],
        compiler_params=_params(("parallel",), 48),
        name="kv_up",
    )(kvn, kr, pos_col, wk, wv_t, invf_row, sgn_row)


def _attn_kernel(qt_ref, k_ref, vt_ref, o_ref, sa_ref, sb_ref, *, tq, tk):
    seq = k_ref.shape[0]
    n_tiles = qt_ref.shape[1] // tq
    n_chunks = seq // tk
    s_bufs = (sa_ref, sb_ref)

    def scores_chunk(t, c, m8):
        rows = slice(c * tk, (c + 1) * tk)
        st = jnp.dot(k_ref[rows, :], qt_ref[:, t * tq:(t + 1) * tq],
                     preferred_element_type=F32)
        s_bufs[t % 2][rows, :] = st
        return jnp.maximum(m8, jnp.max(st.reshape(tk // SUBLANES, SUBLANES, tq), axis=0))

    def probs_chunk(t, c, m, l8, acc):
        rows = slice(c * tk, (c + 1) * tk)
        p = jnp.exp2(s_bufs[t % 2][rows, :] - m)
        l8 = l8 + jnp.sum(p.reshape(tk // SUBLANES, SUBLANES, tq), axis=0)
        acc = acc + jnp.dot(vt_ref[:, rows], p.astype(BF16), preferred_element_type=F32)
        return l8, acc

    neg_inf = jnp.full((SUBLANES, tq), -jnp.inf, F32)
    m8_next = neg_inf
    for c in range(n_chunks):
        m8_next = scores_chunk(0, c, m8_next)
    for t in range(n_tiles):
        m = jnp.max(m8_next, axis=0, keepdims=True)
        m8_next = neg_inf
        l8 = jnp.zeros((SUBLANES, tq), F32)
        acc = jnp.zeros((V_DIM, tq), F32)
        for c in range(n_chunks):
            l8, acc = probs_chunk(t, c, m, l8, acc)
            if t + 1 < n_tiles:
                m8_next = scores_chunk(t + 1, c, m8_next)
        l = jnp.sum(l8, axis=0, keepdims=True)
        o_ref[t * tq:(t + 1) * tq, :] = (acc / l).T.astype(o_ref.dtype)


def _attention(qt, k, vt, batch, seq, tq=512, tk=512, tiles_per_step=4):
    rows = tq * tiles_per_step
    nr = seq // rows
    return pl.pallas_call(
        functools.partial(_attn_kernel, tq=tq, tk=tk),
        grid=(batch, A_HEADS, nr),
        in_specs=[pl.BlockSpec((None, QK_PAD, rows), lambda b, h, i: (b, h, i)),
                  pl.BlockSpec((seq, QK_PAD), lambda b, h, i: (b, h)),
                  pl.BlockSpec((None, V_DIM, seq), lambda b, h, i: (b, h, 0))],
        out_specs=pl.BlockSpec((rows, V_DIM), lambda b, h, i: (b * nr + i, h)),
        out_shape=jax.ShapeDtypeStruct((batch * seq, A_WIDTH), BF16),
        scratch_shapes=[pltpu.VMEM((seq, tq), F32), pltpu.VMEM((seq, tq), F32)],
        compiler_params=_params(("parallel", "parallel", "parallel"), 48),
        name="attention",
    )(qt, k, vt)


def _gating_kernel(uv_ref, lng_ref, lnb_ref, ws_ref, bs_ref, og_ref, o_ref, go_ref):
    v = uv_ref[:, G_WIDTH:].astype(F32)
    mu = jnp.mean(v, axis=-1, keepdims=True)
    vc = v - mu
    vln = (vc * lax.rsqrt(jnp.mean(vc * vc, axis=-1, keepdims=True) + EPS) * lng_ref[...]
           + lnb_ref[...]).astype(BF16)
    rows = uv_ref.shape[0]
    for g in range(G_HEADS):
        cols = slice(g * G_HEAD_DIM, (g + 1) * G_HEAD_DIM)
        w = ws_ref[g]
        b = bs_ref[g]
        for c in range(rows // CHUNK):
            rs = slice(c * CHUNK, (c + 1) * CHUNK)
            s = jnp.dot(w, vln[rs, cols], preferred_element_type=F32) + b
            go_ref[rs, cols] = uv_ref[rs, cols].astype(F32) * s
    o_ref[...] = _rms(go_ref[...], og_ref[...]).astype(o_ref.dtype)


def _gating(uv, ln_g, ln_b, ws, bs, out_g, tm=512):
    m = uv.shape[0]
    const2 = lambda i: (0, 0)
    const3 = lambda i: (0, 0, 0)
    return pl.pallas_call(
        _gating_kernel,
        grid=(m // tm,),
        in_specs=[pl.BlockSpec((tm, 2 * G_WIDTH), lambda i: (i, 0)),
                  pl.BlockSpec((1, G_WIDTH), const2),
                  pl.BlockSpec((1, G_WIDTH), const2),
                  pl.BlockSpec(ws.shape, const3),
                  pl.BlockSpec(bs.shape, const3),
                  pl.BlockSpec((1, G_WIDTH), const2)],
        out_specs=pl.BlockSpec((tm, G_WIDTH), lambda i: (i, 0)),
        out_shape=jax.ShapeDtypeStruct((m, G_WIDTH), BF16),
        scratch_shapes=[pltpu.VMEM((tm, G_WIDTH), F32)],
        compiler_params=_params(("parallel",), 48),
        name="gating",
    )(uv, ln_g, ln_b, ws, bs, out_g)


def _out_proj_kernel(a_ref, gn_ref, ag_ref, w1_ref, w2_ref, o_ref, an_ref):
    @pl.when(pl.program_id(1) == 0)
    def _():
        an_ref[...] = _rms(a_ref[...].astype(F32), ag_ref[...]).astype(an_ref.dtype)

    acc = jnp.dot(an_ref[...], w1_ref[...].astype(BF16), preferred_element_type=F32)
    acc = acc + jnp.dot(gn_ref[...], w2_ref[...].astype(BF16), preferred_element_type=F32)
    o_ref[...] = acc.astype(o_ref.dtype)


def _out_proj(a_out, g_n, a_gain, w_out, tm=1024, tn=512):
    m, ka = a_out.shape
    kg = g_n.shape[1]
    n = w_out.shape[1]
    return pl.pallas_call(
        _out_proj_kernel,
        grid=(m // tm, n // tn),
        in_specs=[pl.BlockSpec((tm, ka), lambda i, j: (i, 0)),
                  pl.BlockSpec((tm, kg), lambda i, j: (i, 0)),
                  pl.BlockSpec((1, ka), lambda i, j: (0, 0)),
                  pl.BlockSpec((ka, tn), lambda i, j: (0, j)),
                  pl.BlockSpec((kg, tn), lambda i, j: (ka // kg, j))],
        out_specs=pl.BlockSpec((tm, tn), lambda i, j: (i, j)),
        out_shape=jax.ShapeDtypeStruct((m, n), BF16),
        scratch_shapes=[pltpu.VMEM((tm, ka), BF16)],
        compiler_params=_params(("parallel", "arbitrary"), 52),
        name="out_proj",
    )(a_out, g_n, a_gain, w_out, w_out)


def _gate_up_kernel(a_ref, wg_ref, wu_ref, o_ref):
    a = a_ref[...]
    g = jnp.dot(a, wg_ref[...].astype(BF16), preferred_element_type=F32)
    u = jnp.dot(a, wu_ref[...].astype(BF16), preferred_element_type=F32)
    o_ref[...] = (jax.nn.silu(g) * u).astype(o_ref.dtype)


def _gate_up(a, wg, wu, tm=1024, tn=256):
    m, k = a.shape
    n = wg.shape[1]
    wspec = pl.BlockSpec((k, tn), lambda i, j: (0, j))
    return pl.pallas_call(
        _gate_up_kernel,
        grid=(m // tm, n // tn),
        in_specs=[pl.BlockSpec((tm, k), lambda i, j: (i, 0)), wspec, wspec],
        out_specs=pl.BlockSpec((tm, tn), lambda i, j: (i, j)),
        out_shape=jax.ShapeDtypeStruct((m, n), BF16),
        compiler_params=_params(("parallel", "parallel"), 52),
        name="gate_up",
    )(a, wg, wu)


def _down_proj_kernel(a_ref, w_ref, o_ref):
    o_ref[...] = jnp.dot(a_ref[...], w_ref[...], preferred_element_type=F32).astype(o_ref.dtype)


def _down_proj(a, w, tm=512, tn=512):
    m, k = a.shape
    n = w.shape[1]
    return pl.pallas_call(
        _down_proj_kernel,
        grid=(m // tm, n // tn),
        in_specs=[pl.BlockSpec((tm, k), lambda i, j: (i, 0)),
                  pl.BlockSpec((k, tn), lambda i, j: (0, j))],
        out_specs=pl.BlockSpec((tm, tn), lambda i, j: (i, j)),
        out_shape=jax.ShapeDtypeStruct((m, n), BF16),
        compiler_params=_params(("parallel", "parallel"), 56),
        name="down_proj",
    )(a, w)


def kernel(x, positions, pre_mix_norm, w_in, q_norm, kv_norm, w_uq, w_ukv, v_ln_gain, v_ln_bias,
           w_spatial, b_spatial, attn_out_norm, gmlp_out_norm, w_out, post_mix_norm, pre_ffn_norm,
           w_gate, w_up, w_down, post_ffn_norm):
    batch, seq, d = x.shape
    tokens = batch * seq
    depth = w_in.shape[0]

    inv_freq = 1.0 / (ROPE_THETA ** (jnp.arange(0, ROPE_DIM, 2, dtype=F32) / ROPE_DIM))
    zeros_half = jnp.zeros((LANES - ROPE_DIM,), F32)
    invf_row = jnp.concatenate([inv_freq, inv_freq, zeros_half])[None, :]
    sgn_row = jnp.concatenate([-jnp.ones((HALF_ROPE,), F32), jnp.ones((HALF_ROPE,), F32),
                               zeros_half])[None, :]
    invf_col = inv_freq[:, None]
    pos_col = positions.reshape(tokens, 1)
    pos_row = positions.reshape(1, tokens)

    xf = x.reshape(tokens, d)
    for l in range(depth):
        wi = w_in[l].astype(BF16)
        o_kv, o_kr, o_u = Q_LORA, Q_LORA + KV_LORA, Q_LORA + KV_LORA + ROPE_DIM
        w_kr = wi[:, o_kr:o_u]
        pad = jnp.zeros((d, LANES - ROPE_DIM), BF16)
        w_all = jnp.concatenate(
            [wi[:, :o_kr], w_kr, pad, w_kr[:, HALF_ROPE:], w_kr[:, :HALF_ROPE], pad,
             jnp.zeros((d, UV_COL0 - QKV_COLS), BF16), wi[:, o_u:]],
            axis=1)
        wuq_t = w_uq[l].T.astype(BF16)
        wkv3 = w_ukv[l].reshape(KV_LORA, A_HEADS, NOPE_DIM + V_DIM)
        wk = wkv3[:, :, :NOPE_DIM].reshape(KV_LORA, A_HEADS * NOPE_DIM).astype(BF16)
        wv_t = wkv3[:, :, NOPE_DIM:].reshape(KV_LORA, A_WIDTH).T.astype(BF16)
        ws = w_spatial[l].astype(BF16)
        bs = b_spatial[l][:, :, None]
        w_d = w_down[l].astype(BF16)
        pre_g = pre_mix_norm[l][None, :]

        qn, kvn, kr, xn = _qkv_down(xf, pre_g, w_all, q_norm[l][None, :], kv_norm[l][None, :])
        qt = _q_up(qn, pos_row, wuq_t, invf_col, batch, seq)
        k, vt = _kv_up(kvn, kr, pos_col, wk, wv_t, invf_row, sgn_row, batch, seq)
        a_out = _attention(qt, k, vt, batch, seq)

        uv = _uv_proj(xn, w_all)
        g_n = _gating(uv, v_ln_gain[l][None, :], v_ln_bias[l][None, :], ws, bs,
                      gmlp_out_norm[l][None, :])

        mix = _out_proj(a_out, g_n, attn_out_norm[l][None, :], w_out[l])
        h, hn = _post_mix(xf, mix, post_mix_norm[l][None, :], pre_ffn_norm[l][None, :])

        act = _gate_up(hn, w_gate[l], w_up[l])
        ffn = _down_proj(act, w_d)
        xf = _post_ffn(h, ffn, post_ffn_norm[l][None, :])

    return xf.reshape(batch, seq, d)
```

```python
import functools
import math

import jax
import jax.numpy as jnp
from jax import lax
from jax.experimental import pallas as pl
from jax.experimental.pallas import tpu as pltpu

F32 = jnp.float32
BF16 = jnp.bfloat16

D_MODEL = 4096
A_HEADS = 16
NOPE_DIM = 128
ROPE_DIM = 64
HALF_ROPE = ROPE_DIM // 2
QK_DIM = NOPE_DIM + ROPE_DIM
V_DIM = 128
A_WIDTH = A_HEADS * V_DIM
Q_LORA = 1024
KV_LORA = 512
ROPE_THETA = 10000.0
G_WIDTH = D_MODEL - A_WIDTH
G_HEADS = 16
G_HEAD_DIM = G_WIDTH // G_HEADS
CHUNK = 128
EPS = 1e-6

LANES = 128
SUBLANES = 8
QK_PAD = 2 * LANES
MIB = 1024 * 1024

QKV_COLS = Q_LORA + KV_LORA + 2 * LANES

Q_PRESCALE = (1.0 / math.sqrt(QK_DIM)) * math.log2(math.e)

_NT = (((1,), (1,)), ((), ()))


def _params(semantics, vmem_mib):
    return pltpu.CompilerParams(dimension_semantics=semantics,
                                vmem_limit_bytes=int(vmem_mib * MIB))


def _rms(x, g):
    return x * lax.rsqrt(jnp.mean(x * x, axis=-1, keepdims=True) + EPS) * g


def _post_mix_kernel(x_ref, mix_ref, g1_ref, g2_ref, h_ref, hn_ref):
    h = x_ref[...] + _rms(mix_ref[...].astype(F32), g1_ref[...])
    h_ref[...] = h
    hn_ref[...] = _rms(h, g2_ref[...]).astype(hn_ref.dtype)


def _post_mix(x, mix, g1, g2, tm=256):
    m, d = x.shape
    row = pl.BlockSpec((tm, d), lambda i: (i, 0))
    gain = pl.BlockSpec((1, d), lambda i: (0, 0))
    return pl.pallas_call(
        _post_mix_kernel,
        grid=(m // tm,),
        in_specs=[row, row, gain, gain],
        out_specs=[row, row],
        out_shape=[jax.ShapeDtypeStruct((m, d), F32),
                   jax.ShapeDtypeStruct((m, d), BF16)],
        compiler_params=_params(("parallel",), 48),
        name="post_mix",
    )(x, mix, g1, g2)


def _post_ffn_kernel(h_ref, f_ref, g_ref, o_ref):
    o_ref[...] = h_ref[...] + _rms(f_ref[...].astype(F32), g_ref[...])


def _post_ffn(h, ffn, g, tm=256):
    m, d = h.shape
    row = pl.BlockSpec((tm, d), lambda i: (i, 0))
    return pl.pallas_call(
        _post_ffn_kernel,
        grid=(m // tm,),
        in_specs=[row, row, pl.BlockSpec((1, d), lambda i: (0, 0))],
        out_specs=row,
        out_shape=jax.ShapeDtypeStruct((m, d), F32),
        compiler_params=_params(("parallel",), 48),
        name="post_ffn",
    )(h, ffn, g)


def _qkv_down_kernel(x_ref, pg_ref, w_ref, qg_ref, kg_ref, qn_ref, kvn_ref, kr_ref, xn_ref):
    xn = _rms(x_ref[...], pg_ref[...]).astype(BF16)
    xn_ref[...] = xn
    acc = jnp.dot(xn, w_ref[...], preferred_element_type=F32)
    qn_ref[...] = _rms(acc[:, :Q_LORA], qg_ref[...]).astype(qn_ref.dtype)
    kvn_ref[...] = _rms(acc[:, Q_LORA:Q_LORA + KV_LORA], kg_ref[...]).astype(kvn_ref.dtype)
    kr_ref[...] = acc[:, Q_LORA + KV_LORA:]


def _qkv_down(x, pre_g, w_qkv, q_g, kv_g, tm=512):
    m, d = x.shape
    n_kr = QKV_COLS - Q_LORA - KV_LORA
    const = lambda i: (0, 0)
    row = lambda i: (i, 0)
    return pl.pallas_call(
        _qkv_down_kernel,
        grid=(m // tm,),
        in_specs=[pl.BlockSpec((tm, d), row),
                  pl.BlockSpec((1, d), const),
                  pl.BlockSpec((d, QKV_COLS), const, pipeline_mode=pl.Buffered(1)),
                  pl.BlockSpec((1, Q_LORA), const),
                  pl.BlockSpec((1, KV_LORA), const)],
        out_specs=[pl.BlockSpec((tm, Q_LORA), row),
                   pl.BlockSpec((tm, KV_LORA), row),
                   pl.BlockSpec((tm, n_kr), row),
                   pl.BlockSpec((tm, d), row)],
        out_shape=[jax.ShapeDtypeStruct((m, Q_LORA), BF16),
                   jax.ShapeDtypeStruct((m, KV_LORA), BF16),
                   jax.ShapeDtypeStruct((m, n_kr), F32),
                   jax.ShapeDtypeStruct((m, d), BF16)],
        compiler_params=_params(("parallel",), 56),
        name="qkv_down",
    )(x, pre_g, w_qkv, q_g, kv_g)


def _uv_proj_kernel(xn_ref, w_ref, o_ref):
    acc = jnp.dot(xn_ref[...], w_ref[...], preferred_element_type=F32)
    o_ref[...] = jax.nn.gelu(acc).astype(o_ref.dtype)


def _uv_proj(xn, w_uv, tm=1024, tn=512):
    m, d = xn.shape
    n = w_uv.shape[1]
    return pl.pallas_call(
        _uv_proj_kernel,
        grid=(m // tm, n // tn),
        in_specs=[pl.BlockSpec((tm, d), lambda i, j: (i, 0)),
                  pl.BlockSpec((d, tn), lambda i, j: (0, j))],
        out_specs=pl.BlockSpec((tm, tn), lambda i, j: (i, j)),
        out_shape=jax.ShapeDtypeStruct((m, n), BF16),
        compiler_params=_params(("parallel", "parallel"), 48),
        name="uv_proj",
    )(xn, w_uv)


def _q_up_kernel(qn_ref, pos_ref, wt_ref, invf_ref, o_ref):
    qt = lax.dot_general(wt_ref[...], qn_ref[...], _NT, preferred_element_type=F32)
    ang = invf_ref[...] * pos_ref[...].astype(F32)
    cos_t = jnp.cos(ang)
    sin_t = jnp.sin(ang)
    zeros = jnp.zeros((QK_PAD - QK_DIM, qt.shape[1]), o_ref.dtype)
    for h in range(A_HEADS):
        src = h * QK_DIM
        dst = h * QK_PAD
        t1 = qt[src + NOPE_DIM:src + NOPE_DIM + HALF_ROPE]
        t2 = qt[src + NOPE_DIM + HALF_ROPE:src + QK_DIM]
        o_ref[dst:dst + NOPE_DIM, :] = (qt[src:src + NOPE_DIM] * Q_PRESCALE).astype(o_ref.dtype)
        o_ref[dst + NOPE_DIM:dst + NOPE_DIM + HALF_ROPE, :] = (
            (t1 * cos_t - t2 * sin_t) * Q_PRESCALE).astype(o_ref.dtype)
        o_ref[dst + NOPE_DIM + HALF_ROPE:dst + QK_DIM, :] = (
            (t2 * cos_t + t1 * sin_t) * Q_PRESCALE).astype(o_ref.dtype)
        o_ref[dst + QK_DIM:dst + QK_PAD, :] = zeros


def _q_up(qn, pos_row, wuq_t, invf_col, batch, seq, tm=512):
    m, r = qn.shape
    spb = seq // tm
    const = lambda i: (0, 0)
    return pl.pallas_call(
        _q_up_kernel,
        grid=(m // tm,),
        in_specs=[pl.BlockSpec((tm, r), lambda i: (i, 0)),
                  pl.BlockSpec((1, tm), lambda i: (0, i)),
                  pl.BlockSpec(wuq_t.shape, const),
                  pl.BlockSpec((HALF_ROPE, 1), const)],
        out_specs=pl.BlockSpec((None, A_HEADS * QK_PAD, tm), lambda i: (i // spb, 0, i % spb)),
        out_shape=jax.ShapeDtypeStruct((batch, A_HEADS * QK_PAD, seq), BF16),
        compiler_params=_params(("parallel",), 48),
        name="q_up",
    )(qn, pos_row, wuq_t, invf_col)


def _kv_up_kernel(kvn_ref, kr_ref, pos_ref, wk_ref, wvt_ref, invf_ref, sgn_ref, k_ref, vt_ref):
    kvn = kvn_ref[...]
    kn = jnp.dot(kvn, wk_ref[...], preferred_element_type=F32)
    vt_ref[...] = lax.dot_general(wvt_ref[...], kvn, _NT,
                                  preferred_element_type=F32).astype(vt_ref.dtype)
    ang = pos_ref[...].astype(F32) * invf_ref[...]
    kr = kr_ref[...]
    rope = (kr[:, :LANES] * jnp.cos(ang)
            + kr[:, LANES:] * (jnp.sin(ang) * sgn_ref[...])).astype(k_ref.dtype)
    for h in range(A_HEADS):
        lo = h * QK_PAD
        k_ref[:, lo:lo + NOPE_DIM] = kn[:, h * NOPE_DIM:(h + 1) * NOPE_DIM].astype(k_ref.dtype)
        k_ref[:, lo + NOPE_DIM:lo + QK_PAD] = rope


def _kv_up(kvn, kr, pos_col, wk, wv_t, invf_row, sgn_row, batch, seq, tm=512):
    m = kvn.shape[0]
    spb = seq // tm
    const = lambda i: (0, 0)
    return pl.pallas_call(
        _kv_up_kernel,
        grid=(m // tm,),
        in_specs=[pl.BlockSpec((tm, KV_LORA), lambda i: (i, 0)),
                  pl.BlockSpec((tm, 2 * LANES), lambda i: (i, 0)),
                  pl.BlockSpec((tm, 1), lambda i: (i, 0)),
                  pl.BlockSpec(wk.shape, const),
                  pl.BlockSpec(wv_t.shape, const),
                  pl.BlockSpec((1, LANES), const),
                  pl.BlockSpec((1, LANES), const)],
        out_specs=[pl.BlockSpec((tm, A_HEADS * QK_PAD), lambda i: (i, 0)),
                   pl.BlockSpec((None, A_WIDTH, tm), lambda i: (i // spb, 0, i % spb))],
        out_shape=[jax.ShapeDtypeStruct((m, A_HEADS * QK_PAD), BF16),
                   jax.ShapeDtypeStruct((batch, A_WIDTH, seq), BF16)],
        compiler_params=_params(("parallel",), 48),
        name="kv_up",
    )(kvn, kr, pos_col, wk, wv_t, invf_row, sgn_row)


def _attn_kernel(qt_ref, k_ref, vt_ref, o_ref, sa_ref, sb_ref, *, tq, tk):
    seq = k_ref.shape[0]
    n_tiles = qt_ref.shape[1] // tq
    n_chunks = seq // tk
    s_bufs = (sa_ref, sb_ref)

    def scores_chunk(t, c, m8):
        rows = slice(c * tk, (c + 1) * tk)
        st = jnp.dot(k_ref[rows, :], qt_ref[:, t * tq:(t + 1) * tq],
                     preferred_element_type=F32)
        s_bufs[t % 2][rows, :] = st
        return jnp.maximum(m8, jnp.max(st.reshape(tk // SUBLANES, SUBLANES, tq), axis=0))

    def probs_chunk(t, c, m, l8, acc):
        rows = slice(c * tk, (c + 1) * tk)
        p = jnp.exp2(s_bufs[t % 2][rows, :] - m)
        l8 = l8 + jnp.sum(p.reshape(tk // SUBLANES, SUBLANES, tq), axis=0)
        acc = acc + jnp.dot(vt_ref[:, rows], p.astype(BF16), preferred_element_type=F32)
        return l8, acc

    neg_inf = jnp.full((SUBLANES, tq), -jnp.inf, F32)
    m8_next = neg_inf
    for c in range(n_chunks):
        m8_next = scores_chunk(0, c, m8_next)
    for t in range(n_tiles):
        m = jnp.max(m8_next, axis=0, keepdims=True)
        m8_next = neg_inf
        l8 = jnp.zeros((SUBLANES, tq), F32)
        acc = jnp.zeros((V_DIM, tq), F32)
        for c in range(n_chunks):
            l8, acc = probs_chunk(t, c, m, l8, acc)
            if t + 1 < n_tiles:
                m8_next = scores_chunk(t + 1, c, m8_next)
        l = jnp.sum(l8, axis=0, keepdims=True)
        o_ref[t * tq:(t + 1) * tq, :] = (acc / l).T.astype(o_ref.dtype)


def _attention(qt, k, vt, batch, seq, tq=512, tk=512, tiles_per_step=4):
    rows = tq * tiles_per_step
    nr = seq // rows
    return pl.pallas_call(
        functools.partial(_attn_kernel, tq=tq, tk=tk),
        grid=(batch, A_HEADS, nr),
        in_specs=[pl.BlockSpec((None, QK_PAD, rows), lambda b, h, i: (b, h, i)),
                  pl.BlockSpec((seq, QK_PAD), lambda b, h, i: (b, h)),
                  pl.BlockSpec((None, V_DIM, seq), lambda b, h, i: (b, h, 0))],
        out_specs=pl.BlockSpec((rows, V_DIM), lambda b, h, i: (b * nr + i, h)),
        out_shape=jax.ShapeDtypeStruct((batch * seq, A_WIDTH), BF16),
        scratch_shapes=[pltpu.VMEM((seq, tq), F32), pltpu.VMEM((seq, tq), F32)],
        compiler_params=_params(("parallel", "parallel", "parallel"), 48),
        name="attention",
    )(qt, k, vt)


def _gating_kernel(uv_ref, lng_ref, lnb_ref, ws_ref, bs_ref, og_ref, o_ref, go_ref):
    v = uv_ref[:, G_WIDTH:].astype(F32)
    mu = jnp.mean(v, axis=-1, keepdims=True)
    vc = v - mu
    vln = (vc * lax.rsqrt(jnp.mean(vc * vc, axis=-1, keepdims=True) + EPS) * lng_ref[...]
           + lnb_ref[...]).astype(BF16)
    rows = uv_ref.shape[0]
    for g in range(G_HEADS):
        cols = slice(g * G_HEAD_DIM, (g + 1) * G_HEAD_DIM)
        w = ws_ref[g]
        b = bs_ref[g]
        for c in range(rows // CHUNK):
            rs = slice(c * CHUNK, (c + 1) * CHUNK)
            s = jnp.dot(w, vln[rs, cols], preferred_element_type=F32) + b
            go_ref[rs, cols] = uv_ref[rs, cols].astype(F32) * s
    o_ref[...] = _rms(go_ref[...], og_ref[...]).astype(o_ref.dtype)


def _gating(uv, ln_g, ln_b, ws, bs, out_g, tm=512):
    m = uv.shape[0]
    const2 = lambda i: (0, 0)
    const3 = lambda i: (0, 0, 0)
    return pl.pallas_call(
        _gating_kernel,
        grid=(m // tm,),
        in_specs=[pl.BlockSpec((tm, 2 * G_WIDTH), lambda i: (i, 0)),
                  pl.BlockSpec((1, G_WIDTH), const2),
                  pl.BlockSpec((1, G_WIDTH), const2),
                  pl.BlockSpec(ws.shape, const3),
                  pl.BlockSpec(bs.shape, const3),
                  pl.BlockSpec((1, G_WIDTH), const2)],
        out_specs=pl.BlockSpec((tm, G_WIDTH), lambda i: (i, 0)),
        out_shape=jax.ShapeDtypeStruct((m, G_WIDTH), BF16),
        scratch_shapes=[pltpu.VMEM((tm, G_WIDTH), F32)],
        compiler_params=_params(("parallel",), 48),
        name="gating",
    )(uv, ln_g, ln_b, ws, bs, out_g)


def _out_proj_kernel(a_ref, gn_ref, ag_ref, w1_ref, w2_ref, o_ref, an_ref):
    @pl.when(pl.program_id(1) == 0)
    def _():
        an_ref[...] = _rms(a_ref[...].astype(F32), ag_ref[...]).astype(an_ref.dtype)

    acc = jnp.dot(an_ref[...], w1_ref[...].astype(BF16), preferred_element_type=F32)
    acc = acc + jnp.dot(gn_ref[...], w2_ref[...].astype(BF16), preferred_element_type=F32)
    o_ref[...] = acc.astype(o_ref.dtype)


def _out_proj(a_out, g_n, a_gain, w_out, tm=1024, tn=512):
    m, ka = a_out.shape
    kg = g_n.shape[1]
    n = w_out.shape[1]
    return pl.pallas_call(
        _out_proj_kernel,
        grid=(m // tm, n // tn),
        in_specs=[pl.BlockSpec((tm, ka), lambda i, j: (i, 0)),
                  pl.BlockSpec((tm, kg), lambda i, j: (i, 0)),
                  pl.BlockSpec((1, ka), lambda i, j: (0, 0)),
                  pl.BlockSpec((ka, tn), lambda i, j: (0, j)),
                  pl.BlockSpec((kg, tn), lambda i, j: (ka // kg, j))],
        out_specs=pl.BlockSpec((tm, tn), lambda i, j: (i, j)),
        out_shape=jax.ShapeDtypeStruct((m, n), BF16),
        scratch_shapes=[pltpu.VMEM((tm, ka), BF16)],
        compiler_params=_params(("parallel", "arbitrary"), 52),
        name="out_proj",
    )(a_out, g_n, a_gain, w_out, w_out)


def _gate_up_kernel(a_ref, wg_ref, wu_ref, o_ref):
    a = a_ref[...]
    g = jnp.dot(a, wg_ref[...].astype(BF16), preferred_element_type=F32)
    u = jnp.dot(a, wu_ref[...].astype(BF16), preferred_element_type=F32)
    o_ref[...] = (jax.nn.silu(g) * u).astype(o_ref.dtype)


def _gate_up(a, wg, wu, tm=1024, tn=256):
    m, k = a.shape
    n = wg.shape[1]
    wspec = pl.BlockSpec((k, tn), lambda i, j: (0, j))
    return pl.pallas_call(
        _gate_up_kernel,
        grid=(m // tm, n // tn),
        in_specs=[pl.BlockSpec((tm, k), lambda i, j: (i, 0)), wspec, wspec],
        out_specs=pl.BlockSpec((tm, tn), lambda i, j: (i, j)),
        out_shape=jax.ShapeDtypeStruct((m, n), BF16),
        compiler_params=_params(("parallel", "parallel"), 52),
        name="gate_up",
    )(a, wg, wu)


def _down_proj_kernel(a_ref, w_ref, o_ref):
    o_ref[...] = jnp.dot(a_ref[...], w_ref[...], preferred_element_type=F32).astype(o_ref.dtype)


def _down_proj(a, w, tm=512, tn=512):
    m, k = a.shape
    n = w.shape[1]
    return pl.pallas_call(
        _down_proj_kernel,
        grid=(m // tm, n // tn),
        in_specs=[pl.BlockSpec((tm, k), lambda i, j: (i, 0)),
                  pl.BlockSpec((k, tn), lambda i, j: (0, j))],
        out_specs=pl.BlockSpec((tm, tn), lambda i, j: (i, j)),
        out_shape=jax.ShapeDtypeStruct((m, n), BF16),
        compiler_params=_params(("parallel", "parallel"), 56),
        name="down_proj",
    )(a, w)


def kernel(x, positions, pre_mix_norm, w_in, q_norm, kv_norm, w_uq, w_ukv, v_ln_gain, v_ln_bias,
           w_spatial, b_spatial, attn_out_norm, gmlp_out_norm, w_out, post_mix_norm, pre_ffn_norm,
           w_gate, w_up, w_down, post_ffn_norm):
    batch, seq, d = x.shape
    tokens = batch * seq
    depth = w_in.shape[0]

    inv_freq = 1.0 / (ROPE_THETA ** (jnp.arange(0, ROPE_DIM, 2, dtype=F32) / ROPE_DIM))
    zeros_half = jnp.zeros((LANES - ROPE_DIM,), F32)
    invf_row = jnp.concatenate([inv_freq, inv_freq, zeros_half])[None, :]
    sgn_row = jnp.concatenate([-jnp.ones((HALF_ROPE,), F32), jnp.ones((HALF_ROPE,), F32),
                               zeros_half])[None, :]
    invf_col = inv_freq[:, None]
    pos_col = positions.reshape(tokens, 1)
    pos_row = positions.reshape(1, tokens)

    xf = x.reshape(tokens, d)
    for l in range(depth):
        wi = w_in[l].astype(BF16)
        o_kv, o_kr, o_u = Q_LORA, Q_LORA + KV_LORA, Q_LORA + KV_LORA + ROPE_DIM
        w_kr = wi[:, o_kr:o_u]
        pad = jnp.zeros((d, LANES - ROPE_DIM), BF16)
        w_qkv = jnp.concatenate(
            [wi[:, :o_kr], w_kr, pad, w_kr[:, HALF_ROPE:], w_kr[:, :HALF_ROPE], pad],
            axis=1)
        w_uv = wi[:, o_u:]
        wuq_t = w_uq[l].T.astype(BF16)
        wkv3 = w_ukv[l].reshape(KV_LORA, A_HEADS, NOPE_DIM + V_DIM)
        wk = wkv3[:, :, :NOPE_DIM].reshape(KV_LORA, A_HEADS * NOPE_DIM).astype(BF16)
        wv_t = wkv3[:, :, NOPE_DIM:].reshape(KV_LORA, A_WIDTH).T.astype(BF16)
        ws = w_spatial[l].astype(BF16)
        bs = b_spatial[l][:, :, None]
        w_d = w_down[l].astype(BF16)
        pre_g = pre_mix_norm[l][None, :]

        qn, kvn, kr, xn = _qkv_down(xf, pre_g, w_qkv, q_norm[l][None, :], kv_norm[l][None, :])
        qt = _q_up(qn, pos_row, wuq_t, invf_col, batch, seq)
        k, vt = _kv_up(kvn, kr, pos_col, wk, wv_t, invf_row, sgn_row, batch, seq)
        a_out = _attention(qt, k, vt, batch, seq)

        uv = _uv_proj(xn, w_uv)
        g_n = _gating(uv, v_ln_gain[l][None, :], v_ln_bias[l][None, :], ws, bs,
                      gmlp_out_norm[l][None, :])

        mix = _out_proj(a_out, g_n, attn_out_norm[l][None, :], w_out[l])
        h, hn = _post_mix(xf, mix, post_mix_norm[l][None, :], pre_ffn_norm[l][None, :])

        act = _gate_up(hn, w_gate[l], w_up[l])
        ffn = _down_proj(act, w_d)
        xf = _post_ffn(h, ffn, post_ffn_norm[l][None, :])

    return xf.reshape(batch, seq, d)
```

```python
import functools
import math

import jax
import jax.numpy as jnp
from jax import lax
from jax.experimental import pallas as pl
from jax.experimental.pallas import tpu as pltpu

F32 = jnp.float32
BF16 = jnp.bfloat16

D_MODEL = 4096
A_HEADS = 16
NOPE_DIM = 128
ROPE_DIM = 64
HALF_ROPE = ROPE_DIM // 2
QK_DIM = NOPE_DIM + ROPE_DIM
V_DIM = 128
A_WIDTH = A_HEADS * V_DIM
Q_LORA = 1024
KV_LORA = 512
ROPE_THETA = 10000.0
G_WIDTH = D_MODEL - A_WIDTH
G_HEADS = 16
G_HEAD_DIM = G_WIDTH // G_HEADS
CHUNK = 128
EPS = 1e-6

LANES = 128
SUBLANES = 8
QK_PAD = 2 * LANES
MIB = 1024 * 1024


Q_PRESCALE = (1.0 / math.sqrt(QK_DIM)) * math.log2(math.e)

_NT = (((1,), (1,)), ((), ()))


def _params(semantics, vmem_mib):
    return pltpu.CompilerParams(dimension_semantics=semantics,
                                vmem_limit_bytes=int(vmem_mib * MIB))


def _rms(x, g):
    return x * lax.rsqrt(jnp.mean(x * x, axis=-1, keepdims=True) + EPS) * g


def _post_mix_kernel(x_ref, mix_ref, g1_ref, g2_ref, h_ref, hn_ref):
    h = x_ref[...] + _rms(mix_ref[...].astype(F32), g1_ref[...])
    h_ref[...] = h
    hn_ref[...] = _rms(h, g2_ref[...]).astype(hn_ref.dtype)


def _post_mix(x, mix, g1, g2, tm=256):
    m, d = x.shape
    row = pl.BlockSpec((tm, d), lambda i: (i, 0))
    gain = pl.BlockSpec((1, d), lambda i: (0, 0))
    return pl.pallas_call(
        _post_mix_kernel,
        grid=(m // tm,),
        in_specs=[row, row, gain, gain],
        out_specs=[row, row],
        out_shape=[jax.ShapeDtypeStruct((m, d), F32),
                   jax.ShapeDtypeStruct((m, d), BF16)],
        compiler_params=_params(("parallel",), 48),
        name="post_mix",
    )(x, mix, g1, g2)


def _post_ffn_kernel(h_ref, f_ref, g_ref, o_ref):
    o_ref[...] = h_ref[...] + _rms(f_ref[...].astype(F32), g_ref[...])


def _post_ffn(h, ffn, g, tm=256):
    m, d = h.shape
    row = pl.BlockSpec((tm, d), lambda i: (i, 0))
    return pl.pallas_call(
        _post_ffn_kernel,
        grid=(m // tm,),
        in_specs=[row, row, pl.BlockSpec((1, d), lambda i: (0, 0))],
        out_specs=row,
        out_shape=jax.ShapeDtypeStruct((m, d), F32),
        compiler_params=_params(("parallel",), 48),
        name="post_ffn",
    )(h, ffn, g)


def _qkv_down_kernel(x_ref, pg_ref, w_ref, wr_ref, qg_ref, kg_ref, qn_ref, kvn_ref, kr_ref, xn_ref):
    xn = _rms(x_ref[...], pg_ref[...]).astype(BF16)
    xn_ref[...] = xn
    acc = jnp.dot(xn, w_ref[...], preferred_element_type=F32)
    qn_ref[...] = _rms(acc[:, :Q_LORA], qg_ref[...]).astype(qn_ref.dtype)
    kvn_ref[...] = _rms(acc[:, Q_LORA:], kg_ref[...]).astype(kvn_ref.dtype)
    kr_ref[...] = jnp.dot(xn, wr_ref[...], preferred_element_type=F32)


def _qkv_down(x, pre_g, w_in_bf, w_kr2, q_g, kv_g, tm=512):
    m, d = x.shape
    n_kr = w_kr2.shape[1]
    const = lambda i: (0, 0)
    row = lambda i: (i, 0)
    return pl.pallas_call(
        _qkv_down_kernel,
        grid=(m // tm,),
        in_specs=[pl.BlockSpec((tm, d), row),
                  pl.BlockSpec((1, d), const),
                  pl.BlockSpec((d, Q_LORA + KV_LORA), const, pipeline_mode=pl.Buffered(1)),
                  pl.BlockSpec((d, n_kr), const),
                  pl.BlockSpec((1, Q_LORA), const),
                  pl.BlockSpec((1, KV_LORA), const)],
        out_specs=[pl.BlockSpec((tm, Q_LORA), row),
                   pl.BlockSpec((tm, KV_LORA), row),
                   pl.BlockSpec((tm, n_kr), row),
                   pl.BlockSpec((tm, d), row)],
        out_shape=[jax.ShapeDtypeStruct((m, Q_LORA), BF16),
                   jax.ShapeDtypeStruct((m, KV_LORA), BF16),
                   jax.ShapeDtypeStruct((m, n_kr), F32),
                   jax.ShapeDtypeStruct((m, d), BF16)],
        compiler_params=_params(("parallel",), 56),
        name="qkv_down",
    )(x, pre_g, w_in_bf, w_kr2, q_g, kv_g)


def _uv_proj_kernel(xn_ref, w_ref, o_ref):
    acc = jnp.dot(xn_ref[...], w_ref[...], preferred_element_type=F32)
    o_ref[...] = jax.nn.gelu(acc).astype(o_ref.dtype)


def _uv_proj(xn, w_uv, tm=1024, tn=512):
    m, d = xn.shape
    n = w_uv.shape[1]
    return pl.pallas_call(
        _uv_proj_kernel,
        grid=(m // tm, n // tn),
        in_specs=[pl.BlockSpec((tm, d), lambda i, j: (i, 0)),
                  pl.BlockSpec((d, tn), lambda i, j: (0, j))],
        out_specs=pl.BlockSpec((tm, tn), lambda i, j: (i, j)),
        out_shape=jax.ShapeDtypeStruct((m, n), BF16),
        compiler_params=_params(("parallel", "parallel"), 48),
        name="uv_proj",
    )(xn, w_uv)


def _q_up_kernel(qn_ref, pos_ref, wt_ref, invf_ref, o_ref):
    qt = lax.dot_general(wt_ref[...], qn_ref[...], _NT, preferred_element_type=F32)
    ang = invf_ref[...] * pos_ref[...].astype(F32)
    cos_t = jnp.cos(ang)
    sin_t = jnp.sin(ang)
    zeros = jnp.zeros((QK_PAD - QK_DIM, qt.shape[1]), o_ref.dtype)
    for h in range(A_HEADS):
        src = h * QK_DIM
        dst = h * QK_PAD
        t1 = qt[src + NOPE_DIM:src + NOPE_DIM + HALF_ROPE]
        t2 = qt[src + NOPE_DIM + HALF_ROPE:src + QK_DIM]
        o_ref[dst:dst + NOPE_DIM, :] = (qt[src:src + NOPE_DIM] * Q_PRESCALE).astype(o_ref.dtype)
        o_ref[dst + NOPE_DIM:dst + NOPE_DIM + HALF_ROPE, :] = (
            (t1 * cos_t - t2 * sin_t) * Q_PRESCALE).astype(o_ref.dtype)
        o_ref[dst + NOPE_DIM + HALF_ROPE:dst + QK_DIM, :] = (
            (t2 * cos_t + t1 * sin_t) * Q_PRESCALE).astype(o_ref.dtype)
        o_ref[dst + QK_DIM:dst + QK_PAD, :] = zeros


def _q_up(qn, pos_row, wuq_t, invf_col, batch, seq, tm=512):
    m, r = qn.shape
    spb = seq // tm
    const = lambda i: (0, 0)
    return pl.pallas_call(
        _q_up_kernel,
        grid=(m // tm,),
        in_specs=[pl.BlockSpec((tm, r), lambda i: (i, 0)),
                  pl.BlockSpec((1, tm), lambda i: (0, i)),
                  pl.BlockSpec(wuq_t.shape, const),
                  pl.BlockSpec((HALF_ROPE, 1), const)],
        out_specs=pl.BlockSpec((None, A_HEADS * QK_PAD, tm), lambda i: (i // spb, 0, i % spb)),
        out_shape=jax.ShapeDtypeStruct((batch, A_HEADS * QK_PAD, seq), BF16),
        compiler_params=_params(("parallel",), 48),
        name="q_up",
    )(qn, pos_row, wuq_t, invf_col)


def _kv_up_kernel(kvn_ref, kr_ref, pos_ref, wk_ref, wvt_ref, invf_ref, sgn_ref, k_ref, vt_ref):
    kvn = kvn_ref[...]
    kn = jnp.dot(kvn, wk_ref[...], preferred_element_type=F32)
    vt_ref[...] = lax.dot_general(wvt_ref[...], kvn, _NT,
                                  preferred_element_type=F32).astype(vt_ref.dtype)
    ang = pos_ref[...].astype(F32) * invf_ref[...]
    kr = kr_ref[...]
    rope = (kr[:, :LANES] * jnp.cos(ang)
            + kr[:, LANES:] * (jnp.sin(ang) * sgn_ref[...])).astype(k_ref.dtype)
    for h in range(A_HEADS):
        lo = h * QK_PAD
        k_ref[:, lo:lo + NOPE_DIM] = kn[:, h * NOPE_DIM:(h + 1) * NOPE_DIM].astype(k_ref.dtype)
        k_ref[:, lo + NOPE_DIM:lo + QK_PAD] = rope


def _kv_up(kvn, kr, pos_col, wk, wv_t, invf_row, sgn_row, batch, seq, tm=512):
    m = kvn.shape[0]
    spb = seq // tm
    const = lambda i: (0, 0)
    return pl.pallas_call(
        _kv_up_kernel,
        grid=(m // tm,),
        in_specs=[pl.BlockSpec((tm, KV_LORA), lambda i: (i, 0)),
                  pl.BlockSpec((tm, 2 * LANES), lambda i: (i, 0)),
                  pl.BlockSpec((tm, 1), lambda i: (i, 0)),
                  pl.BlockSpec(wk.shape, const),
                  pl.BlockSpec(wv_t.shape, const),
                  pl.BlockSpec((1, LANES), const),
                  pl.BlockSpec((1, LANES), const)],
        out_specs=[pl.BlockSpec((tm, A_HEADS * QK_PAD), lambda i: (i, 0)),
                   pl.BlockSpec((None, A_WIDTH, tm), lambda i: (i // spb, 0, i % spb))],
        out_shape=[jax.ShapeDtypeStruct((m, A_HEADS * QK_PAD), BF16),
                   jax.ShapeDtypeStruct((batch, A_WIDTH, seq), BF16)],
        compiler_params=_params(("parallel",), 48),
        name="kv_up",
    )(kvn, kr, pos_col, wk, wv_t, invf_row, sgn_row)


def _attn_kernel(qt_ref, k_ref, vt_ref, o_ref, sa_ref, sb_ref, *, tq, tk):
    seq = k_ref.shape[0]
    n_tiles = qt_ref.shape[1] // tq
    n_chunks = seq // tk
    s_bufs = (sa_ref, sb_ref)

    def scores_chunk(t, c, m8):
        rows = slice(c * tk, (c + 1) * tk)
        st = jnp.dot(k_ref[rows, :], qt_ref[:, t * tq:(t + 1) * tq],
                     preferred_element_type=F32)
        s_bufs[t % 2][rows, :] = st
        return jnp.maximum(m8, jnp.max(st.reshape(tk // SUBLANES, SUBLANES, tq), axis=0))

    def probs_chunk(t, c, m, l8, acc):
        rows = slice(c * tk, (c + 1) * tk)
        p = jnp.exp2(s_bufs[t % 2][rows, :] - m)
        l8 = l8 + jnp.sum(p.reshape(tk // SUBLANES, SUBLANES, tq), axis=0)
        acc = acc + jnp.dot(vt_ref[:, rows], p.astype(BF16), preferred_element_type=F32)
        return l8, acc

    neg_inf = jnp.full((SUBLANES, tq), -jnp.inf, F32)
    m8_next = neg_inf
    for c in range(n_chunks):
        m8_next = scores_chunk(0, c, m8_next)
    for t in range(n_tiles):
        m = jnp.max(m8_next, axis=0, keepdims=True)
        m8_next = neg_inf
        l8 = jnp.zeros((SUBLANES, tq), F32)
        acc = jnp.zeros((V_DIM, tq), F32)
        for c in range(n_chunks):
            l8, acc = probs_chunk(t, c, m, l8, acc)
            if t + 1 < n_tiles:
                m8_next = scores_chunk(t + 1, c, m8_next)
        l = jnp.sum(l8, axis=0, keepdims=True)
        o_ref[t * tq:(t + 1) * tq, :] = (acc / l).T.astype(o_ref.dtype)


def _attention(qt, k, vt, batch, seq, tq=512, tk=512, tiles_per_step=4):
    rows = tq * tiles_per_step
    nr = seq // rows
    return pl.pallas_call(
        functools.partial(_attn_kernel, tq=tq, tk=tk),
        grid=(batch, A_HEADS, nr),
        in_specs=[pl.BlockSpec((None, QK_PAD, rows), lambda b, h, i: (b, h, i)),
                  pl.BlockSpec((seq, QK_PAD), lambda b, h, i: (b, h)),
                  pl.BlockSpec((None, V_DIM, seq), lambda b, h, i: (b, h, 0))],
        out_specs=pl.BlockSpec((rows, V_DIM), lambda b, h, i: (b * nr + i, h)),
        out_shape=jax.ShapeDtypeStruct((batch * seq, A_WIDTH), BF16),
        scratch_shapes=[pltpu.VMEM((seq, tq), F32), pltpu.VMEM((seq, tq), F32)],
        compiler_params=_params(("parallel", "parallel", "parallel"), 48),
        name="attention",
    )(qt, k, vt)


def _gating_kernel(uv_ref, lng_ref, lnb_ref, ws_ref, bs_ref, og_ref, o_ref, go_ref):
    v = uv_ref[:, G_WIDTH:].astype(F32)
    mu = jnp.mean(v, axis=-1, keepdims=True)
    vc = v - mu
    vln = (vc * lax.rsqrt(jnp.mean(vc * vc, axis=-1, keepdims=True) + EPS) * lng_ref[...]
           + lnb_ref[...]).astype(BF16)
    rows = uv_ref.shape[0]
    for g in range(G_HEADS):
        cols = slice(g * G_HEAD_DIM, (g + 1) * G_HEAD_DIM)
        w = ws_ref[g]
        b = bs_ref[g]
        for c in range(rows // CHUNK):
            rs = slice(c * CHUNK, (c + 1) * CHUNK)
            s = jnp.dot(w, vln[rs, cols], preferred_element_type=F32) + b
            go_ref[rs, cols] = uv_ref[rs, cols].astype(F32) * s
    o_ref[...] = _rms(go_ref[...], og_ref[...]).astype(o_ref.dtype)


def _gating(uv, ln_g, ln_b, ws, bs, out_g, tm=512):
    m = uv.shape[0]
    const2 = lambda i: (0, 0)
    const3 = lambda i: (0, 0, 0)
    return pl.pallas_call(
        _gating_kernel,
        grid=(m // tm,),
        in_specs=[pl.BlockSpec((tm, 2 * G_WIDTH), lambda i: (i, 0)),
                  pl.BlockSpec((1, G_WIDTH), const2),
                  pl.BlockSpec((1, G_WIDTH), const2),
                  pl.BlockSpec(ws.shape, const3),
                  pl.BlockSpec(bs.shape, const3),
                  pl.BlockSpec((1, G_WIDTH), const2)],
        out_specs=pl.BlockSpec((tm, G_WIDTH), lambda i: (i, 0)),
        out_shape=jax.ShapeDtypeStruct((m, G_WIDTH), BF16),
        scratch_shapes=[pltpu.VMEM((tm, G_WIDTH), F32)],
        compiler_params=_params(("parallel",), 48),
        name="gating",
    )(uv, ln_g, ln_b, ws, bs, out_g)


def _out_proj_kernel(a_ref, gn_ref, ag_ref, w1_ref, w2_ref, o_ref, an_ref):
    @pl.when(pl.program_id(1) == 0)
    def _():
        an_ref[...] = _rms(a_ref[...].astype(F32), ag_ref[...]).astype(an_ref.dtype)

    acc = jnp.dot(an_ref[...], w1_ref[...].astype(BF16), preferred_element_type=F32)
    acc = acc + jnp.dot(gn_ref[...], w2_ref[...].astype(BF16), preferred_element_type=F32)
    o_ref[...] = acc.astype(o_ref.dtype)


def _out_proj(a_out, g_n, a_gain, w_out, tm=1024, tn=512):
    m, ka = a_out.shape
    kg = g_n.shape[1]
    n = w_out.shape[1]
    return pl.pallas_call(
        _out_proj_kernel,
        grid=(m // tm, n // tn),
        in_specs=[pl.BlockSpec((tm, ka), lambda i, j: (i, 0)),
                  pl.BlockSpec((tm, kg), lambda i, j: (i, 0)),
                  pl.BlockSpec((1, ka), lambda i, j: (0, 0)),
                  pl.BlockSpec((ka, tn), lambda i, j: (0, j)),
                  pl.BlockSpec((kg, tn), lambda i, j: (ka // kg, j))],
        out_specs=pl.BlockSpec((tm, tn), lambda i, j: (i, j)),
        out_shape=jax.ShapeDtypeStruct((m, n), BF16),
        scratch_shapes=[pltpu.VMEM((tm, ka), BF16)],
        compiler_params=_params(("parallel", "arbitrary"), 52),
        name="out_proj",
    )(a_out, g_n, a_gain, w_out, w_out)


def _gate_up_kernel(a_ref, wg_ref, wu_ref, o_ref):
    a = a_ref[...]
    g = jnp.dot(a, wg_ref[...].astype(BF16), preferred_element_type=F32)
    u = jnp.dot(a, wu_ref[...].astype(BF16), preferred_element_type=F32)
    o_ref[...] = (jax.nn.silu(g) * u).astype(o_ref.dtype)


def _gate_up(a, wg, wu, tm=1024, tn=256):
    m, k = a.shape
    n = wg.shape[1]
    wspec = pl.BlockSpec((k, tn), lambda i, j: (0, j))
    return pl.pallas_call(
        _gate_up_kernel,
        grid=(m // tm, n // tn),
        in_specs=[pl.BlockSpec((tm, k), lambda i, j: (i, 0)), wspec, wspec],
        out_specs=pl.BlockSpec((tm, tn), lambda i, j: (i, j)),
        out_shape=jax.ShapeDtypeStruct((m, n), BF16),
        compiler_params=_params(("parallel", "parallel"), 52),
        name="gate_up",
    )(a, wg, wu)


def _down_proj_kernel(a_ref, w_ref, o_ref):
    o_ref[...] = jnp.dot(a_ref[...], w_ref[...], preferred_element_type=F32).astype(o_ref.dtype)


def _down_proj(a, w, tm=512, tn=512):
    m, k = a.shape
    n = w.shape[1]
    return pl.pallas_call(
        _down_proj_kernel,
        grid=(m // tm, n // tn),
        in_specs=[pl.BlockSpec((tm, k), lambda i, j: (i, 0)),
                  pl.BlockSpec((k, tn), lambda i, j: (0, j))],
        out_specs=pl.BlockSpec((tm, tn), lambda i, j: (i, j)),
        out_shape=jax.ShapeDtypeStruct((m, n), BF16),
        compiler_params=_params(("parallel", "parallel"), 56),
        name="down_proj",
    )(a, w)


def kernel(x, positions, pre_mix_norm, w_in, q_norm, kv_norm, w_uq, w_ukv, v_ln_gain, v_ln_bias,
           w_spatial, b_spatial, attn_out_norm, gmlp_out_norm, w_out, post_mix_norm, pre_ffn_norm,
           w_gate, w_up, w_down, post_ffn_norm):
    batch, seq, d = x.shape
    tokens = batch * seq
    depth = w_in.shape[0]

    inv_freq = 1.0 / (ROPE_THETA ** (jnp.arange(0, ROPE_DIM, 2, dtype=F32) / ROPE_DIM))
    zeros_half = jnp.zeros((LANES - ROPE_DIM,), F32)
    invf_row = jnp.concatenate([inv_freq, inv_freq, zeros_half])[None, :]
    sgn_row = jnp.concatenate([-jnp.ones((HALF_ROPE,), F32), jnp.ones((HALF_ROPE,), F32),
                               zeros_half])[None, :]
    invf_col = inv_freq[:, None]
    pos_col = positions.reshape(tokens, 1)
    pos_row = positions.reshape(1, tokens)

    xf = x.reshape(tokens, d)
    for l in range(depth):
        wi = w_in[l].astype(BF16)
        o_kv, o_kr, o_u = Q_LORA, Q_LORA + KV_LORA, Q_LORA + KV_LORA + ROPE_DIM
        w_kr = wi[:, o_kr:o_u]
        pad = jnp.zeros((d, LANES - ROPE_DIM), BF16)
        w_kr2 = jnp.concatenate(
            [w_kr, pad, w_kr[:, HALF_ROPE:], w_kr[:, :HALF_ROPE], pad], axis=1)
        w_uv = wi[:, o_u:]
        wuq_t = w_uq[l].T.astype(BF16)
        wkv3 = w_ukv[l].reshape(KV_LORA, A_HEADS, NOPE_DIM + V_DIM)
        wk = wkv3[:, :, :NOPE_DIM].reshape(KV_LORA, A_HEADS * NOPE_DIM).astype(BF16)
        wv_t = wkv3[:, :, NOPE_DIM:].reshape(KV_LORA, A_WIDTH).T.astype(BF16)
        ws = w_spatial[l].astype(BF16)
        bs = b_spatial[l][:, :, None]
        w_d = w_down[l].astype(BF16)
        pre_g = pre_mix_norm[l][None, :]

        qn, kvn, kr, xn = _qkv_down(xf, pre_g, wi, w_kr2, q_norm[l][None, :], kv_norm[l][None, :])
        qt = _q_up(qn, pos_row, wuq_t, invf_col, batch, seq)
        k, vt = _kv_up(kvn, kr, pos_col, wk, wv_t, invf_row, sgn_row, batch, seq)
        a_out = _attention(qt, k, vt, batch, seq)

        uv = _uv_proj(xn, w_uv)
        g_n = _gating(uv, v_ln_gain[l][None, :], v_ln_bias[l][None, :], ws, bs,
                      gmlp_out_norm[l][None, :])

        mix = _out_proj(a_out, g_n, attn_out_norm[l][None, :], w_out[l])
        h, hn = _post_mix(xf, mix, post_mix_norm[l][None, :], pre_ffn_norm[l][None, :])

        act = _gate_up(hn, w_gate[l], w_up[l])
        ffn = _down_proj(act, w_d)
        xf = _post_ffn(h, ffn, post_ffn_norm[l][None, :])

    return xf.reshape(batch, seq, d)
```
